```python
import jax, jax.numpy as jnp
from jax import lax
import numpy as np

D_MODEL = 2048
BATCH = 4
SEQ = 2048
DEPTH = 4
DEC_BATCH = 32
DEC_SEQ = 8
PAST_LEN = 16384
PAGE_SIZE = 128

N_MIXERS = 3
RMS_EPS = 1e-6
A_HEADS = 32
A_KV_HEADS = 4
A_HEAD_DIM = 64
A_GROUP = A_HEADS // A_KV_HEADS
A_Q_DIM = A_HEADS * A_HEAD_DIM
A_KV_DIM = A_KV_HEADS * A_HEAD_DIM
A_IN_DIM = 2 * A_Q_DIM + 2 * A_KV_DIM
WINDOW = 128
ATTN_BLOCK = 128
ROT_DIM = A_HEAD_DIM // 4
ROPE_THETA = 500000.0
CONV_WIDTH = 3
B_WIDTH = D_MODEL
B_IN_DIM = 4 * B_WIDTH
C_HEADS = 16
C_KEY_DIM = 128
C_VAL_DIM = D_MODEL // C_HEADS
C_F_DIM = C_HEADS * C_KEY_DIM
C_I_DIM = C_HEADS * C_VAL_DIM
C_IN_DIM = 2 * C_F_DIM + C_I_DIM + D_MODEL
C_CHUNK = 32

N_A_LAYERS = (DEPTH + 2) // 3
N_B_LAYERS = (DEPTH + 1) // 3
N_C_LAYERS = DEPTH // 3

kernel_name = "hybrid_swa_conv_hgrn2_step"


def rmsnorm(x, g):
    xf = x.astype(jnp.float32)
    y = xf * lax.rsqrt(jnp.mean(xf * xf, axis=-1, keepdims=True) + RMS_EPS)
    return (y * g.astype(jnp.float32)).astype(x.dtype)


def partial_rope(x, pos):
    half = ROT_DIM // 2
    inv = ROPE_THETA ** (-jnp.arange(half, dtype=jnp.float32) * 2.0 / ROT_DIM)
    ang = pos.astype(jnp.float32)[:, None] * inv[None, :]
    cos = jnp.cos(ang)[None, :, None, :]
    sin = jnp.sin(ang)[None, :, None, :]
    xf = x.astype(jnp.float32)
    x1 = xf[..., :half]
    x2 = xf[..., half:ROT_DIM]
    out = jnp.concatenate([x1 * cos - x2 * sin, x2 * cos + x1 * sin, xf[..., ROT_DIM:]], axis=-1)
    return out.astype(x.dtype)


def sink_attend(q, k, v, mask, sinks):
    s = jnp.einsum("...qhgd,...khd->...hgqk", q.astype(jnp.float32), k.astype(jnp.float32)) * (A_HEAD_DIM ** -0.5)
    s = jnp.where(mask, s, -jnp.inf)
    sk = sinks.astype(jnp.float32).reshape(A_KV_HEADS, A_GROUP, 1, 1)
    m = jnp.maximum(jnp.max(s, axis=-1, keepdims=True), sk)
    p = jnp.exp(s - m)
    denom = jnp.sum(p, axis=-1, keepdims=True) + jnp.exp(sk - m)
    o = jnp.einsum("...hgqk,...khd->...qhgd", p / denom, v.astype(jnp.float32))
    return o.astype(q.dtype)


def window_attn_prompt(q, k, v, sinks):
    bsz, s = q.shape[:2]
    nb = s // ATTN_BLOCK
    qb = q.reshape(bsz, nb, ATTN_BLOCK, A_KV_HEADS, A_GROUP, A_HEAD_DIM)
    kb = k.reshape(bsz, nb, ATTN_BLOCK, A_KV_HEADS, A_HEAD_DIM)
    vb = v.reshape(bsz, nb, ATTN_BLOCK, A_KV_HEADS, A_HEAD_DIM)
    prev = lambda t: jnp.concatenate([jnp.zeros_like(t[:, :1]), t[:, :-1]], axis=1)
    kk = jnp.concatenate([prev(kb), kb], axis=2)
    vv = jnp.concatenate([prev(vb), vb], axis=2)
    blk = jnp.arange(nb)[:, None]
    qpos = blk * ATTN_BLOCK + jnp.arange(ATTN_BLOCK)[None, :]
    kpos = (blk - 1) * ATTN_BLOCK + jnp.arange(2 * ATTN_BLOCK)[None, :]
    diff = qpos[:, :, None] - kpos[:, None, :]
    mask = (diff >= 0) & (diff < WINDOW) & (kpos[:, None, :] >= 0)
    o = sink_attend(qb, kk, vv, mask[:, None, None], sinks)
    return o.reshape(bsz, s, A_HEADS, A_HEAD_DIM)


def window_attn_sample(q, k_all, v_all, sinks):
    bsz, t = q.shape[:2]
    w = k_all.shape[1] - t
    qpos = w + jnp.arange(t)
    kpos = jnp.arange(w + t)
    diff = qpos[:, None] - kpos[None, :]
    mask = (diff >= 0) & (diff < WINDOW)
    qg = q.reshape(bsz, t, A_KV_HEADS, A_GROUP, A_HEAD_DIM)
    o = sink_attend(qg, k_all, v_all, mask, sinks)
    return o.reshape(bsz, t, A_HEADS, A_HEAD_DIM)


def attn_mixer(h, w_in, w_out, sinks, pos, past_k, past_v):
    bsz, t, _ = h.shape
    proj = h @ w_in
    q, k, v, z = jnp.split(proj, [A_Q_DIM, A_Q_DIM + A_KV_DIM, A_Q_DIM + 2 * A_KV_DIM], axis=-1)
    q = partial_rope(q.reshape(bsz, t, A_HEADS, A_HEAD_DIM), pos)
    k = partial_rope(k.reshape(bsz, t, A_KV_HEADS, A_HEAD_DIM), pos)
    v = v.reshape(bsz, t, A_KV_HEADS, A_HEAD_DIM)
    if past_k is None:
        o = window_attn_prompt(q, k, v, sinks)
        k_all, v_all = k, v
    else:
        k_all = jnp.concatenate([past_k.astype(k.dtype), k], axis=1)
        v_all = jnp.concatenate([past_v.astype(v.dtype), v], axis=1)
        o = window_attn_sample(q, k_all, v_all, sinks)
    w = min(WINDOW, k_all.shape[1])
    y = (jax.nn.silu(z) * o.reshape(bsz, t, A_Q_DIM)) @ w_out
    return y, k_all[:, -w:], v_all[:, -w:]


def conv_mixer(h, w_in, conv_w, w_out, past_u):
    bsz, t, _ = h.shape
    proj = h @ w_in
    b_gate, c_gate, xin, z = jnp.split(proj, 4, axis=-1)
    u = c_gate * xin
    if past_u is None:
        past_u = jnp.zeros((bsz, CONV_WIDTH - 1, B_WIDTH), u.dtype)
    u_pad = jnp.concatenate([past_u.astype(u.dtype), u], axis=1)
    conv = conv_w[0] * u_pad[:, 0:t]
    for j in range(1, CONV_WIDTH):
        conv = conv + conv_w[j] * u_pad[:, j:j + t]
    y = (jax.nn.silu(z) * b_gate * conv) @ w_out
    return y, u_pad[:, -(CONV_WIDTH - 1):]


def gla_chunked(q, k, v, log_f, s0):
    bsz, t, nh, dk = q.shape
    n = -(-t // C_CHUNK)
    pad = n * C_CHUNK - t
    padf = lambda a: jnp.pad(a, ((0, 0), (0, pad), (0, 0), (0, 0)))
    to_chunks = lambda a: jnp.moveaxis(a.reshape(bsz, n, C_CHUNK, nh, a.shape[-1]), 1, 0)
    xs = tuple(to_chunks(padf(a)) for a in (q, k, v, log_f))
    causal = jnp.tril(jnp.ones((C_CHUNK, C_CHUNK), dtype=bool))

    def step(S, inp):
        qc, kc, vc, lc = inp
        b = jnp.cumsum(lc, axis=1)
        b_last = b[:, -1]
        q_t = qc * jnp.exp(b)
        k_intra = kc * jnp.exp(-b)
        k_state = kc * jnp.exp(b_last[:, None] - b)
        o_inter = jnp.einsum("bchk,bhkv->bchv", q_t, S)
        att = jnp.where(causal, jnp.einsum("bthk,bshk->bhts", q_t, k_intra), 0.0)
        o_intra = jnp.einsum("bhts,bshv->bthv", att, vc)
        S_new = jnp.exp(b_last)[..., None] * S + jnp.einsum("bchk,bchv->bhkv", k_state, vc)
        return S_new, o_inter + o_intra

    s_t, o = lax.scan(step, s0.astype(jnp.float32), xs)
    o = jnp.moveaxis(o, 0, 1).reshape(bsz, n * C_CHUNK, nh, v.shape[-1])[:, :t]
    return o, s_t


def hgrn2_mixer(h, w_in, norm_g, w_out, lb, s0):
    bsz, t, _ = h.shape
    proj = h @ w_in
    q, fl, i, z = jnp.split(proj, [C_F_DIM, 2 * C_F_DIM, 2 * C_F_DIM + C_I_DIM], axis=-1)
    lbf = lb.astype(jnp.float32)
    f = lbf + (1.0 - lbf) * jax.nn.sigmoid(fl.astype(jnp.float32))
    log_f = jnp.log(f)
    k = 1.0 - f
    q = jax.nn.silu(q.astype(jnp.float32))
    shp = (bsz, t, C_HEADS, C_KEY_DIM)
    if s0 is None:
        s0 = jnp.zeros((bsz, C_HEADS, C_KEY_DIM, C_VAL_DIM), jnp.float32)
    o, s_t = gla_chunked(q.reshape(shp), k.reshape(shp),
                         i.astype(jnp.float32).reshape(bsz, t, C_HEADS, C_VAL_DIM),
                         log_f.reshape(shp), s0)
    o = rmsnorm(o, norm_g).reshape(bsz, t, C_I_DIM).astype(h.dtype)
    y = (jax.nn.silu(z) * o) @ w_out
    return y, s_t


def trunk(x, pos, cache_k, cache_v, state_conv, state_hgrn, ln_g, final_g,
          a_w_in, a_w_out, a_sinks, b_w_in, b_conv_w, b_w_out,
          c_w_in, c_norm_g, c_w_out, c_lb_logits):
    lb_cum = jnp.cumsum(jax.nn.softmax(c_lb_logits.astype(jnp.float32), axis=0), axis=0)
    lb_all = lb_cum - lb_cum[0]
    ks, vs, us, ss = [], [], [], []
    for li in range(DEPTH):
        h = rmsnorm(x, ln_g[li])
        kind = li % N_MIXERS
        j = li // N_MIXERS
        if kind == 0:
            pk = None if cache_k is None else cache_k[j]
            pv = None if cache_v is None else cache_v[j]
            y, nk, nv = attn_mixer(h, a_w_in[j], a_w_out[j], a_sinks[j], pos, pk, pv)
            ks.append(nk)
            vs.append(nv)
        elif kind == 1:
            pu = None if state_conv is None else state_conv[j]
            y, nu = conv_mixer(h, b_w_in[j], b_conv_w[j], b_w_out[j], pu)
            us.append(nu)
        else:
            ps = None if state_hgrn is None else state_hgrn[j]
            y, ns = hgrn2_mixer(h, c_w_in[j], c_norm_g[j], c_w_out[j], lb_all[li], ps)
            ss.append(ns)
        x = x + y
    return rmsnorm(x, final_g), jnp.stack(ks), jnp.stack(vs), jnp.stack(us), jnp.stack(ss)


def setup_inputs(seed: int = 0) -> dict:
    key = jax.random.key(seed)
    ks = jax.random.split(key, 20)
    nrm = lambda k, shape, scale: jax.random.normal(k, shape, jnp.float32) * scale
    win_rows = min(WINDOW, PAST_LEN)
    return {
        "x_prompt": nrm(ks[0], (BATCH, SEQ, D_MODEL), 1.0),
        "x_sample": nrm(ks[1], (DEC_BATCH, DEC_SEQ, D_MODEL), 1.0),
        "cache_k": nrm(ks[2], (N_A_LAYERS, DEC_BATCH, win_rows, A_KV_HEADS, A_HEAD_DIM), 1.0),
        "cache_v": nrm(ks[3], (N_A_LAYERS, DEC_BATCH, win_rows, A_KV_HEADS, A_HEAD_DIM), 1.0),
        "state_conv": nrm(ks[4], (N_B_LAYERS, DEC_BATCH, CONV_WIDTH - 1, B_WIDTH), 1.0),
        "state_hgrn": nrm(ks[5], (N_C_LAYERS, DEC_BATCH, C_HEADS, C_KEY_DIM, C_VAL_DIM), 0.4),
        "ln_g": 1.0 + nrm(ks[6], (DEPTH, D_MODEL), 0.02),
        "final_g": 1.0 + nrm(ks[7], (D_MODEL,), 0.02),
        "a_w_in": nrm(ks[8], (N_A_LAYERS, D_MODEL, A_IN_DIM), D_MODEL ** -0.5),
        "a_w_out": nrm(ks[9], (N_A_LAYERS, A_Q_DIM, D_MODEL), A_Q_DIM ** -0.5),
        "a_sinks": nrm(ks[10], (N_A_LAYERS, A_HEADS), 0.5),
        "b_w_in": nrm(ks[11], (N_B_LAYERS, D_MODEL, B_IN_DIM), D_MODEL ** -0.5),
        "b_conv_w": nrm(ks[12], (N_B_LAYERS, CONV_WIDTH, B_WIDTH), CONV_WIDTH ** -0.5),
        "b_w_out": nrm(ks[13], (N_B_LAYERS, B_WIDTH, D_MODEL), B_WIDTH ** -0.5),
        "c_w_in": nrm(ks[14], (N_C_LAYERS, D_MODEL, C_IN_DIM), D_MODEL ** -0.5),
        "c_norm_g": 1.0 + nrm(ks[15], (N_C_LAYERS, C_VAL_DIM), 0.02),
        "c_w_out": nrm(ks[16], (N_C_LAYERS, C_I_DIM, D_MODEL), C_I_DIM ** -0.5),
        "c_lb_logits": nrm(ks[17], (DEPTH, C_F_DIM), 0.1),
    }


def reference(x_prompt, x_sample, cache_k, cache_v, state_conv, state_hgrn, ln_g, final_g,
              a_w_in, a_w_out, a_sinks, b_w_in, b_conv_w, b_w_out,
              c_w_in, c_norm_g, c_w_out, c_lb_logits):
    pos_prompt = jnp.arange(x_prompt.shape[1], dtype=jnp.int32)
    pos_sample = PAST_LEN + jnp.arange(x_sample.shape[1], dtype=jnp.int32)
    y_prompt, k_p, v_p, conv_p, hgrn_p = trunk(
        x_prompt, pos_prompt, None, None, None, None, ln_g, final_g,
        a_w_in, a_w_out, a_sinks, b_w_in, b_conv_w, b_w_out,
        c_w_in, c_norm_g, c_w_out, c_lb_logits)
    y_sample, k_s, v_s, conv_s, hgrn_s = trunk(
        x_sample, pos_sample, cache_k, cache_v, state_conv, state_hgrn, ln_g, final_g,
        a_w_in, a_w_out, a_sinks, b_w_in, b_conv_w, b_w_out,
        c_w_in, c_norm_g, c_w_out, c_lb_logits)
    return (y_prompt, y_sample, k_p, v_p, k_s, v_s, conv_p, conv_s, hgrn_p, hgrn_s)
```

```python
import functools

import jax
import jax.numpy as jnp
from jax import lax
from jax.experimental import pallas as pl
from jax.experimental.pallas import tpu as pltpu

D_MODEL = 2048
BATCH = 4
SEQ = 2048
DEPTH = 4
DEC_BATCH = 32
DEC_SEQ = 8
PAST_LEN = 16384
RMS_EPS = 1e-6

A_HEADS = 32
A_KV_HEADS = 4
A_HEAD_DIM = 64
A_GROUP = A_HEADS // A_KV_HEADS
A_Q_DIM = A_HEADS * A_HEAD_DIM
A_KV_DIM = A_KV_HEADS * A_HEAD_DIM
A_IN_DIM = 2 * A_Q_DIM + 2 * A_KV_DIM
WINDOW = 128
ROT_DIM = A_HEAD_DIM // 4
ROPE_THETA = 500000.0

CONV_WIDTH = 3
B_IN_DIM = 4 * D_MODEL

C_HEADS = 16
C_KEY_DIM = 128
C_VAL_DIM = D_MODEL // C_HEADS
C_F_DIM = C_HEADS * C_KEY_DIM
C_IN_DIM = 2 * C_F_DIM + 2 * D_MODEL
C_CHUNK = 32

LANES = 128
SUBLANES = 8
P_ROWS = BATCH * SEQ
S_ROWS = DEC_BATCH * DEC_SEQ
ROWS = P_ROWS + S_ROWS
ROW_TILE = 256
SAMPLE_SEQS = 2
MM_ROW_TILE = ROWS // 8
MM_COL_TILE = 1024
NEG_BIG = -1e30
VMEM_LIMIT = 56 * 1024 * 1024

_f32 = jnp.float32
_bf16 = jnp.bfloat16


def _cparams(n_axes):
    return pltpu.CompilerParams(dimension_semantics=("arbitrary",) * n_axes, vmem_limit_bytes=VMEM_LIMIT)


def _silu(x):
    return x * jax.nn.sigmoid(x)


def _rms_scale(x, g):
    ms = jnp.mean(x * x, axis=-1, keepdims=True)
    return x * lax.rsqrt(ms + RMS_EPS) * g


def _first_norm_kernel(xp_ref, xs_ref, g_ref, h_ref):
    m = pl.program_id(0)
    x = jnp.where(m < P_ROWS // ROW_TILE, xp_ref[...], xs_ref[...])
    h_ref[...] = _rms_scale(x, g_ref[...]).astype(h_ref.dtype)


def _first_norm(xp, xs, g):
    n_p = P_ROWS // ROW_TILE
    return pl.pallas_call(
        _first_norm_kernel,
        grid=(ROWS // ROW_TILE,),
        in_specs=[
            pl.BlockSpec((ROW_TILE, D_MODEL), lambda m: (jnp.minimum(m, n_p - 1), 0)),
            pl.BlockSpec((ROW_TILE, D_MODEL), lambda m: (0, 0)),
            pl.BlockSpec((1, D_MODEL), lambda m: (0, 0)),
        ],
        out_specs=pl.BlockSpec((ROW_TILE, D_MODEL), lambda m: (m, 0)),
        out_shape=jax.ShapeDtypeStruct((ROWS, D_MODEL), _bf16),
        compiler_params=_cparams(1),
        name="first_norm",
    )(xp, xs, g)


def _in_proj_kernel(h_ref, w_ref, o_ref):
    o_ref[...] = jnp.dot(h_ref[...], w_ref[...], preferred_element_type=_f32)


def _in_proj(h, w):
    n = w.shape[1]
    tn = MM_COL_TILE if n % MM_COL_TILE == 0 else 512
    return pl.pallas_call(
        _in_proj_kernel,
        grid=(n // tn, ROWS // MM_ROW_TILE),
        in_specs=[
            pl.BlockSpec((MM_ROW_TILE, D_MODEL), lambda j, m: (m, 0)),
            pl.BlockSpec((D_MODEL, tn), lambda j, m: (0, j)),
        ],
        out_specs=pl.BlockSpec((MM_ROW_TILE, tn), lambda j, m: (m, j)),
        out_shape=jax.ShapeDtypeStruct((ROWS, n), _f32),
        compiler_params=_cparams(2),
        name="in_proj",
    )(h, w)


def _out_proj_kernel(a_ref, w_ref, x_ref, g_ref, xn_ref, h_ref):
    xn = x_ref[...] + jnp.dot(a_ref[...], w_ref[...], preferred_element_type=_f32)
    xn_ref[...] = xn
    h_ref[...] = _rms_scale(xn, g_ref[...]).astype(h_ref.dtype)


def _out_proj_first_kernel(a_ref, w_ref, xp_ref, xs_ref, g_ref, xn_ref, h_ref):
    m = pl.program_id(0)
    x = jnp.where(m < P_ROWS // ROW_TILE, xp_ref[...], xs_ref[...])
    xn = x + jnp.dot(a_ref[...], w_ref[...], preferred_element_type=_f32)
    xn_ref[...] = xn
    h_ref[...] = _rms_scale(xn, g_ref[...]).astype(h_ref.dtype)


def _out_proj_last_kernel(a_ref, w_ref, x_ref, g_ref, yp_ref, ys_ref):
    m = pl.program_id(0)
    xn = x_ref[...] + jnp.dot(a_ref[...], w_ref[...], preferred_element_type=_f32)
    y = _rms_scale(xn, g_ref[...])

    @pl.when(m < P_ROWS // ROW_TILE)
    def _():
        yp_ref[...] = y

    @pl.when(m >= P_ROWS // ROW_TILE)
    def _():
        ys_ref[...] = y


def _row_spec(width):
    return pl.BlockSpec((ROW_TILE, width), lambda m: (m, 0))


def _out_proj(a, w, x, g):
    return pl.pallas_call(
        _out_proj_kernel,
        grid=(ROWS // ROW_TILE,),
        in_specs=[_row_spec(D_MODEL), pl.BlockSpec((D_MODEL, D_MODEL), lambda m: (0, 0)),
                  _row_spec(D_MODEL), pl.BlockSpec((1, D_MODEL), lambda m: (0, 0))],
        out_specs=[_row_spec(D_MODEL), _row_spec(D_MODEL)],
        out_shape=[jax.ShapeDtypeStruct((ROWS, D_MODEL), _f32), jax.ShapeDtypeStruct((ROWS, D_MODEL), _bf16)],
        compiler_params=_cparams(1),
        name="out_proj",
    )(a, w, x, g)


def _out_proj_first(a, w, xp, xs, g):
    n_p = P_ROWS // ROW_TILE
    return pl.pallas_call(
        _out_proj_first_kernel,
        grid=(ROWS // ROW_TILE,),
        in_specs=[_row_spec(D_MODEL), pl.BlockSpec((D_MODEL, D_MODEL), lambda m: (0, 0)),
                  pl.BlockSpec((ROW_TILE, D_MODEL), lambda m: (jnp.minimum(m, n_p - 1), 0)),
                  pl.BlockSpec((ROW_TILE, D_MODEL), lambda m: (0, 0)),
                  pl.BlockSpec((1, D_MODEL), lambda m: (0, 0))],
        out_specs=[_row_spec(D_MODEL), _row_spec(D_MODEL)],
        out_shape=[jax.ShapeDtypeStruct((ROWS, D_MODEL), _f32), jax.ShapeDtypeStruct((ROWS, D_MODEL), _bf16)],
        compiler_params=_cparams(1),
        name="out_proj_first",
    )(a, w, xp, xs, g)


def _out_proj_last(a, w, x, g):
    n_p = P_ROWS // ROW_TILE
    return pl.pallas_call(
        _out_proj_last_kernel,
        grid=(ROWS // ROW_TILE,),
        in_specs=[_row_spec(D_MODEL), pl.BlockSpec((D_MODEL, D_MODEL), lambda m: (0, 0)),
                  _row_spec(D_MODEL), pl.BlockSpec((1, D_MODEL), lambda m: (0, 0))],
        out_specs=[pl.BlockSpec((ROW_TILE, D_MODEL), lambda m: (jnp.minimum(m, n_p - 1), 0)),
                   pl.BlockSpec((ROW_TILE, D_MODEL), lambda m: (0, 0))],
        out_shape=[jax.ShapeDtypeStruct((P_ROWS, D_MODEL), _f32), jax.ShapeDtypeStruct((S_ROWS, D_MODEL), _f32)],
        compiler_params=_cparams(1),
        name="out_proj_last",
    )(a, w, x, g)


def _rope_tables(pos):
    half = ROT_DIM // 2
    inv = ROPE_THETA ** (-jnp.arange(half, dtype=_f32) * 2.0 / ROT_DIM)
    ang = pos.astype(_f32)[:, None] * inv[None, :]
    cos, sin = jnp.cos(ang), jnp.sin(ang)
    t = pos.shape[0]
    rest = A_HEAD_DIM - ROT_DIM
    c_head = jnp.concatenate([cos, cos, jnp.ones((t, rest), _f32)], axis=1)
    s1_head = jnp.concatenate([-sin, jnp.zeros((t, half + rest), _f32)], axis=1)
    s2_head = jnp.concatenate([jnp.zeros((t, half), _f32), sin, jnp.zeros((t, rest), _f32)], axis=1)
    rep = LANES // A_HEAD_DIM
    return jnp.stack([jnp.tile(c_head, (1, rep)), jnp.tile(s1_head, (1, rep)), jnp.tile(s2_head, (1, rep))])


def _rope(x, tabs):
    half = ROT_DIM // 2
    outs = []
    for c in range(x.shape[1] // LANES):
        xc = x[:, c * LANES:(c + 1) * LANES]
        outs.append(xc * tabs[0] + pltpu.roll(xc, LANES - half, 1) * tabs[1] + pltpu.roll(xc, half, 1) * tabs[2])
    return outs


def _lane_halves(xc, head_in_low_half):
    lane = lax.broadcasted_iota(jnp.int32, xc.shape, 1)
    low = lane < A_HEAD_DIM
    sw = pltpu.roll(xc, A_HEAD_DIM, 1)
    if head_in_low_half:
        return jnp.where(low, xc, 0.0), jnp.where(low, 0.0, sw)
    return jnp.where(low, sw, 0.0), jnp.where(low, 0.0, xc)


def _attn_core(q_cols, kk, vv, bias, sinks_ref, z_cols):
    nt = (((1,), (1,)), ((), ()))
    cols_per_kv = A_GROUP * A_HEAD_DIM // LANES
    tq = q_cols[0].shape[0]
    lane = lax.broadcasted_iota(jnp.int32, (tq, LANES), 1)
    low = lane < A_HEAD_DIM
    out = []
    for j in range(A_KV_HEADS):
        kc = kk[:, (j // 2) * LANES:(j // 2 + 1) * LANES]
        vc = vv[:, (j // 2) * LANES:(j // 2 + 1) * LANES]
        k_lo, k_hi = _lane_halves(kc, j % 2 == 0)
        v_lo, v_hi = _lane_halves(vc, j % 2 == 0)
        qs = jnp.concatenate(q_cols[j * cols_per_kv:(j + 1) * cols_per_kv], axis=0).astype(_bf16)
        s_lo = lax.dot_general(qs, k_lo.astype(_bf16), nt, preferred_element_type=_f32)
        s_hi = lax.dot_general(qs, k_hi.astype(_bf16), nt, preferred_element_type=_f32)
        p_lo, p_hi, r_lo, r_hi = [], [], [], []
        for c in range(cols_per_kv):
            for s_all, ps, rs, odd in ((s_lo, p_lo, r_lo, 0), (s_hi, p_hi, r_hi, 1)):
                sink = sinks_ref[j * A_GROUP + 2 * c + odd]
                s = s_all[c * tq:(c + 1) * tq] + bias
                m = jnp.maximum(jnp.max(s, axis=-1, keepdims=True), sink)
                p = jnp.exp(s - m)
                den = jnp.sum(p, axis=-1, keepdims=True) + jnp.exp(sink - m)
                ps.append(p)
                rs.append(1.0 / den)
        o = (jnp.dot(jnp.concatenate(p_lo, axis=0).astype(_bf16), v_lo.astype(_bf16), preferred_element_type=_f32)
             + jnp.dot(jnp.concatenate(p_hi, axis=0).astype(_bf16), v_hi.astype(_bf16), preferred_element_type=_f32))
        for c in range(cols_per_kv):
            oc = o[c * tq:(c + 1) * tq] * jnp.where(low, r_lo[c], r_hi[c])
            out.append(_silu(z_cols[j * cols_per_kv + c]) * oc)
    return out


def _lane_chunks(ref, start, width, rows=slice(None)):
    return [ref[rows, start + c * LANES:start + (c + 1) * LANES] for c in range(width // LANES)]


def _attn_prompt_kernel(sinks_ref, proj_ref, tab_ref, bias_ref, g_ref, ko_ref, vo_ref, kprev, vprev):
    i = pl.program_id(1)

    @pl.when(i == 0)
    def _():
        kprev[...] = jnp.zeros_like(kprev)
        vprev[...] = jnp.zeros_like(vprev)

    tabs = tab_ref[...]
    q_cols = [c * (A_HEAD_DIM ** -0.5) for c in _rope(proj_ref[:, :A_Q_DIM], tabs)]
    k_new = jnp.concatenate(_rope(proj_ref[:, A_Q_DIM:A_Q_DIM + A_KV_DIM], tabs), axis=1)
    v_new = proj_ref[:, A_Q_DIM + A_KV_DIM:A_Q_DIM + 2 * A_KV_DIM]
    ko_ref[0] = k_new
    vo_ref[0] = v_new
    kk = jnp.concatenate([kprev[...], k_new], axis=0)
    vv = jnp.concatenate([vprev[...], v_new], axis=0)
    z_cols = _lane_chunks(proj_ref, A_Q_DIM + 2 * A_KV_DIM, A_Q_DIM)
    g_cols = _attn_core(q_cols, kk, vv, bias_ref[0], sinks_ref, z_cols)
    for c, gc in enumerate(g_cols):
        g_ref[:, c * LANES:(c + 1) * LANES] = gc.astype(g_ref.dtype)
    kprev[...] = k_new
    vprev[...] = v_new


def _prompt_bias():
    r = jnp.arange(WINDOW)[:, None]
    c = jnp.arange(2 * WINDOW)[None, :]
    band = (c > r) & (c <= r + WINDOW)
    first = band & (c >= WINDOW)
    return jnp.where(jnp.stack([first, band]), 0.0, NEG_BIG).astype(_f32)


def _attn_prompt(proj, sinks, tabs):
    nb = SEQ // WINDOW
    return pl.pallas_call(
        _attn_prompt_kernel,
        grid=(BATCH, nb),
        in_specs=[
            pl.BlockSpec(memory_space=pltpu.SMEM),
            pl.BlockSpec((WINDOW, A_IN_DIM), lambda b, i: (b * nb + i, 0)),
            pl.BlockSpec((3, WINDOW, LANES), lambda b, i: (0, i, 0)),
            pl.BlockSpec((1, WINDOW, 2 * WINDOW), lambda b, i: (jnp.minimum(i, 1), 0, 0)),
        ],
        out_specs=[
            pl.BlockSpec((WINDOW, A_Q_DIM), lambda b, i: (b * nb + i, 0)),
            pl.BlockSpec((1, WINDOW, A_KV_DIM), lambda b, i: (b, 0, 0)),
            pl.BlockSpec((1, WINDOW, A_KV_DIM), lambda b, i: (b, 0, 0)),
        ],
        out_shape=[
            jax.ShapeDtypeStruct((ROWS, A_Q_DIM), _bf16),
            jax.ShapeDtypeStruct((BATCH, WINDOW, A_KV_DIM), _f32),
            jax.ShapeDtypeStruct((BATCH, WINDOW, A_KV_DIM), _f32),
        ],
        scratch_shapes=[pltpu.VMEM((WINDOW, A_KV_DIM), _f32), pltpu.VMEM((WINDOW, A_KV_DIM), _f32)],
        compiler_params=_cparams(2),
        name="attn_prompt",
    )(sinks, proj, tabs, _prompt_bias())


def _attn_sample_kernel(sinks_ref, proj_ref, tab_ref, bias_ref, ck_ref, cv_ref, gin_ref, g_ref, ko_ref, vo_ref,
                        kk_ref, vv_ref):
    del gin_ref
    tabs = tab_ref[...]
    q_all = [c * (A_HEAD_DIM ** -0.5) for c in _rope(proj_ref[:, :A_Q_DIM], tabs)]
    k_all = jnp.concatenate(_rope(proj_ref[:, A_Q_DIM:A_Q_DIM + A_KV_DIM], tabs), axis=1)
    v_all = proj_ref[:, A_Q_DIM + A_KV_DIM:A_Q_DIM + 2 * A_KV_DIM]
    keep = WINDOW - DEC_SEQ
    g_parts = []
    for s in range(SAMPLE_SEQS):
        rows = slice(s * DEC_SEQ, (s + 1) * DEC_SEQ)
        k_new, v_new = k_all[rows], v_all[rows]
        ko_ref[s, :keep] = ck_ref[s, DEC_SEQ:]
        ko_ref[s, keep:] = k_new
        vo_ref[s, :keep] = cv_ref[s, DEC_SEQ:]
        vo_ref[s, keep:] = v_new
        for ref, cache, new in ((kk_ref, ck_ref, k_new), (vv_ref, cv_ref, v_new)):
            ref[:WINDOW] = cache[s]
            ref[WINDOW:WINDOW + DEC_SEQ] = new
            ref[WINDOW + DEC_SEQ:] = jnp.zeros((WINDOW - DEC_SEQ, A_KV_DIM), _f32)
        z_cols = _lane_chunks(proj_ref, A_Q_DIM + 2 * A_KV_DIM, A_Q_DIM, rows)
        g_parts.append(_attn_core([q[rows] for q in q_all], kk_ref[...], vv_ref[...], bias_ref[...],
                                  sinks_ref, z_cols))
    for c in range(A_Q_DIM // LANES):
        g_ref[:, c * LANES:(c + 1) * LANES] = jnp.concatenate([p[c] for p in g_parts], axis=0).astype(g_ref.dtype)


def _sample_bias():
    t = jnp.arange(DEC_SEQ)[:, None]
    c = jnp.arange(2 * WINDOW)[None, :]
    ok = (c > t) & (c <= t + WINDOW)
    return jnp.where(ok, 0.0, NEG_BIG).astype(_f32)


def _attn_sample(proj, sinks, tabs, cache_k, cache_v, g):
    srows = SAMPLE_SEQS * DEC_SEQ
    row0 = P_ROWS // srows
    return pl.pallas_call(
        _attn_sample_kernel,
        grid=(DEC_BATCH // SAMPLE_SEQS,),
        in_specs=[
            pl.BlockSpec(memory_space=pltpu.SMEM),
            pl.BlockSpec((srows, A_IN_DIM), lambda b: (row0 + b, 0)),
            pl.BlockSpec((3, srows, LANES), lambda b: (0, 0, 0)),
            pl.BlockSpec((DEC_SEQ, 2 * WINDOW), lambda b: (0, 0)),
            pl.BlockSpec((SAMPLE_SEQS, WINDOW, A_KV_DIM), lambda b: (b, 0, 0)),
            pl.BlockSpec((SAMPLE_SEQS, WINDOW, A_KV_DIM), lambda b: (b, 0, 0)),
            pl.BlockSpec(memory_space=pl.ANY),
        ],
        out_specs=[
            pl.BlockSpec((srows, A_Q_DIM), lambda b: (row0 + b, 0)),
            pl.BlockSpec((SAMPLE_SEQS, WINDOW, A_KV_DIM), lambda b: (b, 0, 0)),
            pl.BlockSpec((SAMPLE_SEQS, WINDOW, A_KV_DIM), lambda b: (b, 0, 0)),
        ],
        out_shape=[
            jax.ShapeDtypeStruct((ROWS, A_Q_DIM), _bf16),
            jax.ShapeDtypeStruct((DEC_BATCH, WINDOW, A_KV_DIM), _f32),
            jax.ShapeDtypeStruct((DEC_BATCH, WINDOW, A_KV_DIM), _f32),
        ],
        scratch_shapes=[pltpu.VMEM((2 * WINDOW, A_KV_DIM), _f32), pltpu.VMEM((2 * WINDOW, A_KV_DIM), _f32)],
        input_output_aliases={6: 0},
        compiler_params=_cparams(1),
        name="attn_sample",
    )(sinks, proj, tabs, _sample_bias(), cache_k, cache_v, g)


CONV_COLS = 256


def _conv_taps(u, u1, u2, w_ref):
    return w_ref[0:1, :] * u2 + w_ref[1:2, :] * u1 + w_ref[2:3, :] * u


def _conv_prompt_kernel(bg_ref, cg_ref, xi_ref, z_ref, w_ref, g_ref, tail_ref):
    u = cg_ref[...] * xi_ref[...]
    row = lax.broadcasted_iota(jnp.int32, u.shape, 0)
    u1 = jnp.where(row >= 1, pltpu.roll(u, 1, 0), 0.0)
    u2 = jnp.where(row >= 2, pltpu.roll(u, 2, 0), 0.0)
    conv = _conv_taps(u, u1, u2, w_ref)
    g_ref[...] = (_silu(z_ref[...]) * bg_ref[...] * conv).astype(g_ref.dtype)
    tail_ref[0] = u[SEQ - SUBLANES:]


def _conv_prompt(proj, conv_w):
    ncb = D_MODEL // CONV_COLS

    def seg(k):
        return pl.BlockSpec((SEQ, CONV_COLS), lambda b, c: (b, k * ncb + c))

    return pl.pallas_call(
        _conv_prompt_kernel,
        grid=(BATCH, ncb),
        in_specs=[seg(0), seg(1), seg(2), seg(3), pl.BlockSpec((CONV_WIDTH, CONV_COLS), lambda b, c: (0, c))],
        out_specs=[pl.BlockSpec((SEQ, CONV_COLS), lambda b, c: (b, c)),
                   pl.BlockSpec((1, SUBLANES, CONV_COLS), lambda b, c: (b, 0, c))],
        out_shape=[jax.ShapeDtypeStruct((ROWS, D_MODEL), _bf16),
                   jax.ShapeDtypeStruct((BATCH, SUBLANES, D_MODEL), _f32)],
        compiler_params=_cparams(2),
        name="conv_prompt",
    )(proj, proj, proj, proj, conv_w)


def _conv_sample_kernel(bg_ref, cg_ref, xi_ref, z_ref, w_ref, past_ref, gin_ref, g_ref, u_ref):
    del gin_ref
    u = cg_ref[...] * xi_ref[...]
    past = past_ref[...]
    t = lax.broadcasted_iota(jnp.int32, u.shape, 0) % DEC_SEQ
    u1 = jnp.where(t >= 1, pltpu.roll(u, 1, 0), pltpu.roll(past, S_ROWS - 1, 0))
    u2 = jnp.where(t >= 2, pltpu.roll(u, 2, 0), past)
    conv = _conv_taps(u, u1, u2, w_ref)
    g_ref[...] = (_silu(z_ref[...]) * bg_ref[...] * conv).astype(g_ref.dtype)
    u_ref[...] = u


def _conv_sample(proj, conv_w, past, g):
    ncb = D_MODEL // CONV_COLS
    rb = P_ROWS // S_ROWS

    def seg(k):
        return pl.BlockSpec((S_ROWS, CONV_COLS), lambda c: (rb, k * ncb + c))

    return pl.pallas_call(
        _conv_sample_kernel,
        grid=(ncb,),
        in_specs=[seg(0), seg(1), seg(2), seg(3), pl.BlockSpec((CONV_WIDTH, CONV_COLS), lambda c: (0, c)),
                  pl.BlockSpec((S_ROWS, CONV_COLS), lambda c: (0, c)),
                  pl.BlockSpec(memory_space=pl.ANY)],
        out_specs=[pl.BlockSpec((S_ROWS, CONV_COLS), lambda c: (rb, c)),
                   pl.BlockSpec((S_ROWS, CONV_COLS), lambda c: (0, c))],
        out_shape=[jax.ShapeDtypeStruct((ROWS, D_MODEL), _bf16),
                   jax.ShapeDtypeStruct((S_ROWS, D_MODEL), _f32)],
        input_output_aliases={6: 0},
        compiler_params=_cparams(1),
        name="conv_sample",
    )(proj, proj, proj, proj, conv_w, past, g)


HG_ROWS = 256


def _lower_bound(logit_ref, layer):
    lg = logit_ref[...]
    e = jnp.exp(lg - jnp.max(lg, axis=0, keepdims=True))
    return jnp.sum(e[1:layer + 1], axis=0, keepdims=True) / jnp.sum(e, axis=0, keepdims=True)


def _hgrn_gates(q, fl, lb):
    f = lb + (1.0 - lb) * jax.nn.sigmoid(fl)
    return _silu(q), 1.0 - f, jnp.log(f)


def _chunk_cumsum(x, chunk):
    t = lax.broadcasted_iota(jnp.int32, x.shape, 0) % chunk
    s = 1
    while s < chunk:
        x = x + jnp.where(t >= s, pltpu.roll(x, s, 0), 0.0)
        s *= 2
    return x


def _hgrn_out(o, z, ng):
    ms = jnp.mean(o * o, axis=-1, keepdims=True)
    return _silu(z) * (o * lax.rsqrt(ms + RMS_EPS) * ng)


def _hgrn_prompt_kernel(layer, q_ref, f_ref, i_ref, z_ref, lg_ref, ng_ref, g_ref, so_ref, st_ref):
    tt = pl.program_id(2)

    @pl.when(tt == 0)
    def _():
        st_ref[...] = jnp.zeros_like(st_ref)

    nt = (((1,), (1,)), ((), ()))
    lb = _lower_bound(lg_ref, layer)
    qq, kk, lf = _hgrn_gates(q_ref[...], f_ref[...], lb)
    b = _chunk_cumsum(lf, C_CHUNK)
    v = i_ref[...]
    row = lax.broadcasted_iota(jnp.int32, (C_CHUNK, C_CHUNK), 0)
    col = lax.broadcasted_iota(jnp.int32, (C_CHUNK, C_CHUNK), 1)
    causal = row >= col
    st = st_ref[...]
    outs = []
    for c in range(HG_ROWS // C_CHUNK):
        sl = slice(c * C_CHUNK, (c + 1) * C_CHUNK)
        bc = b[sl]
        b_last = bc[C_CHUNK - 1:C_CHUNK]
        q_t = (qq[sl] * jnp.exp(bc)).astype(_bf16)
        k_intra = (kk[sl] * jnp.exp(-bc)).astype(_bf16)
        k_state = (kk[sl] * jnp.exp(b_last - bc)).astype(_bf16)
        vc = v[sl].astype(_bf16)
        o_inter = lax.dot_general(q_t, st.astype(_bf16), nt, preferred_element_type=_f32)
        att = jnp.where(causal, lax.dot_general(q_t, k_intra, nt, preferred_element_type=_f32), 0.0)
        o_intra = jnp.dot(att.astype(_bf16), vc, preferred_element_type=_f32)
        outs.append(o_inter + o_intra)
        upd = jnp.dot(v[sl].T.astype(_bf16), k_state, preferred_element_type=_f32)
        st = st * jnp.exp(b_last) + upd
    st_ref[...] = st
    o = jnp.concatenate(outs, axis=0)
    g_ref[...] = _hgrn_out(o, z_ref[...], ng_ref[...]).astype(g_ref.dtype)

    @pl.when(tt == pl.num_programs(2) - 1)
    def _():
        so_ref[0, 0] = st.T


def _hgrn_prompt(proj, lb_logits, norm_g, layer):
    nt = SEQ // HG_ROWS

    def seg(k):
        return pl.BlockSpec((HG_ROWS, LANES), lambda b, h, t: (b * nt + t, k * C_HEADS + h))

    return pl.pallas_call(
        functools.partial(_hgrn_prompt_kernel, layer),
        grid=(BATCH, C_HEADS, nt),
        in_specs=[seg(0), seg(1), seg(2), seg(3),
                  pl.BlockSpec((DEPTH, LANES), lambda b, h, t: (0, h)),
                  pl.BlockSpec((1, C_VAL_DIM), lambda b, h, t: (0, 0))],
        out_specs=[pl.BlockSpec((HG_ROWS, LANES), lambda b, h, t: (b * nt + t, h)),
                   pl.BlockSpec((1, 1, C_KEY_DIM, C_VAL_DIM), lambda b, h, t: (b, h, 0, 0))],
        out_shape=[jax.ShapeDtypeStruct((ROWS, D_MODEL), _bf16),
                   jax.ShapeDtypeStruct((BATCH, C_HEADS, C_KEY_DIM, C_VAL_DIM), _f32)],
        scratch_shapes=[pltpu.VMEM((C_VAL_DIM, C_KEY_DIM), _f32)],
        compiler_params=_cparams(3),
        name="hgrn_prompt",
    )(proj, proj, proj, proj, lb_logits, norm_g)


def _hgrn_sample_kernel(layer, proj_ref, lg_ref, ng_ref, s_ref, gin_ref, g_ref, so_ref):
    del gin_ref
    nt = (((1,), (1,)), ((), ()))
    tn = (((0,), (0,)), ((), ()))
    row = lax.broadcasted_iota(jnp.int32, (DEC_SEQ, DEC_SEQ), 0)
    col = lax.broadcasted_iota(jnp.int32, (DEC_SEQ, DEC_SEQ), 1)
    causal = row >= col
    eye = (lax.broadcasted_iota(jnp.int32, (C_KEY_DIM, C_KEY_DIM), 0)
           == lax.broadcasted_iota(jnp.int32, (C_KEY_DIM, C_KEY_DIM), 1))
    for h in range(C_HEADS):
        hs = slice(h * LANES, (h + 1) * LANES)
        lb = _lower_bound(lg_ref.at[:, hs], layer)
        qq, kk, lf = _hgrn_gates(proj_ref[:, h * LANES:(h + 1) * LANES],
                                 proj_ref[:, C_F_DIM + h * LANES:C_F_DIM + (h + 1) * LANES], lb)
        v = proj_ref[:, 2 * C_F_DIM + h * LANES:2 * C_F_DIM + (h + 1) * LANES]
        z = proj_ref[:, 2 * C_F_DIM + D_MODEL + h * LANES:2 * C_F_DIM + D_MODEL + (h + 1) * LANES]
        b = _chunk_cumsum(lf, DEC_SEQ)
        q_t = qq * jnp.exp(b)
        k_intra = kk * jnp.exp(-b)
        outs = []
        for s in range(SAMPLE_SEQS):
            rows = slice(s * DEC_SEQ, (s + 1) * DEC_SEQ)
            b_last = b[(s + 1) * DEC_SEQ - 1:(s + 1) * DEC_SEQ]
            qb = q_t[rows].astype(_bf16)
            k_state = (kk[rows] * jnp.exp(b_last - b[rows])).astype(_bf16)
            vb = v[rows].astype(_bf16)
            s0 = s_ref[s, h]
            o_inter = jnp.dot(qb, s0.astype(_bf16), preferred_element_type=_f32)
            att = jnp.where(causal, lax.dot_general(qb, k_intra[rows].astype(_bf16), nt,
                                                    preferred_element_type=_f32), 0.0)
            outs.append(o_inter + jnp.dot(att.astype(_bf16), vb, preferred_element_type=_f32))
            decay = jnp.sum(jnp.where(eye, jnp.broadcast_to(jnp.exp(b_last), eye.shape), 0.0),
                            axis=1, keepdims=True)
            so_ref[s, h] = decay * s0 + lax.dot_general(k_state, vb, tn, preferred_element_type=_f32)
        g_ref[:, hs] = _hgrn_out(jnp.concatenate(outs, axis=0), z, ng_ref[...]).astype(g_ref.dtype)


def _hgrn_sample(proj, lb_logits, norm_g, state, g, layer):
    srows = SAMPLE_SEQS * DEC_SEQ
    row0 = P_ROWS // srows
    return pl.pallas_call(
        functools.partial(_hgrn_sample_kernel, layer),
        grid=(DEC_BATCH // SAMPLE_SEQS,),
        in_specs=[pl.BlockSpec((srows, C_IN_DIM), lambda b: (row0 + b, 0)),
                  pl.BlockSpec((DEPTH, C_F_DIM), lambda b: (0, 0)),
                  pl.BlockSpec((1, C_VAL_DIM), lambda b: (0, 0)),
                  pl.BlockSpec((SAMPLE_SEQS, C_HEADS, C_KEY_DIM, C_VAL_DIM), lambda b: (b, 0, 0, 0)),
                  pl.BlockSpec(memory_space=pl.ANY)],
        out_specs=[pl.BlockSpec((srows, D_MODEL), lambda b: (row0 + b, 0)),
                   pl.BlockSpec((SAMPLE_SEQS, C_HEADS, C_KEY_DIM, C_VAL_DIM), lambda b: (b, 0, 0, 0))],
        out_shape=[jax.ShapeDtypeStruct((ROWS, D_MODEL), _bf16),
                   jax.ShapeDtypeStruct((DEC_BATCH, C_HEADS, C_KEY_DIM, C_VAL_DIM), _f32)],
        input_output_aliases={4: 0},
        compiler_params=_cparams(1),
        name="hgrn_sample",
    )(proj, lb_logits, norm_g, state, g)


def kernel(x_prompt, x_sample, cache_k, cache_v, state_conv, state_hgrn, ln_g, final_g, a_w_in, a_w_out, a_sinks,
           b_w_in, b_conv_w, b_w_out, c_w_in, c_norm_g, c_w_out, c_lb_logits):
    xp = x_prompt.reshape(P_ROWS, D_MODEL)
    xs = x_sample.reshape(S_ROWS, D_MODEL)
    tabs_p = _rope_tables(jnp.arange(SEQ, dtype=jnp.int32))
    tabs_s = _rope_tables(PAST_LEN + jnp.arange(SAMPLE_SEQS * DEC_SEQ, dtype=jnp.int32) % DEC_SEQ)
    norm_gains = [ln_g[li].reshape(1, D_MODEL) for li in range(1, DEPTH)] + [final_g.reshape(1, D_MODEL)]

    h = _first_norm(xp, xs, ln_g[0].reshape(1, D_MODEL))
    x = None
    ks_p, vs_p, ks_s, vs_s, conv_p, conv_s, hg_p, hg_s = [], [], [], [], [], [], [], []
    for li in range(DEPTH):
        kind, j = li % 3, li // 3
        if kind == 0:
            proj = _in_proj(h, a_w_in[j].astype(_bf16))
            g, k_p, v_p = _attn_prompt(proj, a_sinks[j], tabs_p)
            g, k_s, v_s = _attn_sample(proj, a_sinks[j], tabs_s,
                                       cache_k[j].reshape(DEC_BATCH, WINDOW, A_KV_DIM),
                                       cache_v[j].reshape(DEC_BATCH, WINDOW, A_KV_DIM), g)
            ks_p.append(k_p), vs_p.append(v_p), ks_s.append(k_s), vs_s.append(v_s)
            w_out = a_w_out[j]
        elif kind == 1:
            proj = _in_proj(h, b_w_in[j].astype(_bf16))
            g, tail = _conv_prompt(proj, b_conv_w[j])
            past = jnp.pad(state_conv[j], ((0, 0), (0, DEC_SEQ - (CONV_WIDTH - 1)), (0, 0))).reshape(S_ROWS, D_MODEL)
            g, u_s = _conv_sample(proj, b_conv_w[j], past, g)
            conv_p.append(tail[:, SUBLANES - (CONV_WIDTH - 1):])
            conv_s.append(u_s.reshape(DEC_BATCH, DEC_SEQ, D_MODEL)[:, DEC_SEQ - (CONV_WIDTH - 1):])
            w_out = b_w_out[j]
        else:
            proj = _in_proj(h, c_w_in[j].astype(_bf16))
            ng = c_norm_g[j].reshape(1, C_VAL_DIM)
            g, s_p = _hgrn_prompt(proj, c_lb_logits, ng, li)
            g, s_s = _hgrn_sample(proj, c_lb_logits, ng, state_hgrn[j], g, li)
            hg_p.append(s_p), hg_s.append(s_s)
            w_out = c_w_out[j]
        w_out = w_out.astype(_bf16)
        if li == 0:
            x, h = _out_proj_first(g, w_out, xp, xs, norm_gains[li])
        elif li < DEPTH - 1:
            x, h = _out_proj(g, w_out, x, norm_gains[li])
        else:
            y_p, y_s = _out_proj_last(g, w_out, x, norm_gains[li])

    kv_p = lambda xs_: jnp.stack(xs_).reshape(len(xs_), BATCH, WINDOW, A_KV_HEADS, A_HEAD_DIM)
    kv_s = lambda xs_: jnp.stack(xs_).reshape(len(xs_), DEC_BATCH, WINDOW, A_KV_HEADS, A_HEAD_DIM)
    return (y_p.reshape(BATCH, SEQ, D_MODEL), y_s.reshape(DEC_BATCH, DEC_SEQ, D_MODEL),
            kv_p(ks_p), kv_p(vs_p), kv_s(ks_s), kv_s(vs_s),
            jnp.stack(conv_p), jnp.stack(conv_s), jnp.stack(hg_p), jnp.stack(hg_s))
```

```python
import functools

import jax
import jax.numpy as jnp
from jax import lax
from jax.experimental import pallas as pl
from jax.experimental.pallas import tpu as pltpu

D_MODEL = 2048
BATCH = 4
SEQ = 2048
DEPTH = 4
DEC_BATCH = 32
DEC_SEQ = 8
PAST_LEN = 16384
RMS_EPS = 1e-6

A_HEADS = 32
A_KV_HEADS = 4
A_HEAD_DIM = 64
A_GROUP = A_HEADS // A_KV_HEADS
A_Q_DIM = A_HEADS * A_HEAD_DIM
A_KV_DIM = A_KV_HEADS * A_HEAD_DIM
A_IN_DIM = 2 * A_Q_DIM + 2 * A_KV_DIM
WINDOW = 128
ROT_DIM = A_HEAD_DIM // 4
ROPE_THETA = 500000.0

CONV_WIDTH = 3
B_IN_DIM = 4 * D_MODEL

C_HEADS = 16
C_KEY_DIM = 128
C_VAL_DIM = D_MODEL // C_HEADS
C_F_DIM = C_HEADS * C_KEY_DIM
C_IN_DIM = 2 * C_F_DIM + 2 * D_MODEL
C_CHUNK = 32

LANES = 128
SUBLANES = 8
P_ROWS = BATCH * SEQ
S_ROWS = DEC_BATCH * DEC_SEQ
ROWS = P_ROWS + S_ROWS
ROW_TILE = 256
SAMPLE_SEQS = 2
MM_ROW_TILE = ROWS // 8
MM_COL_TILE = 2048
NEG_BIG = -1e30
VMEM_LIMIT = 56 * 1024 * 1024

_f32 = jnp.float32
_bf16 = jnp.bfloat16


def _cparams(n_axes):
    return pltpu.CompilerParams(dimension_semantics=("arbitrary",) * n_axes, vmem_limit_bytes=VMEM_LIMIT)


def _silu(x):
    return x * jax.nn.sigmoid(x)


def _rms_scale(x, g):
    ms = jnp.mean(x * x, axis=-1, keepdims=True)
    return x * lax.rsqrt(ms + RMS_EPS) * g


def _first_norm_kernel(xp_ref, xs_ref, g_ref, h_ref):
    m = pl.program_id(0)
    x = jnp.where(m < P_ROWS // ROW_TILE, xp_ref[...], xs_ref[...])
    h_ref[...] = _rms_scale(x, g_ref[...]).astype(h_ref.dtype)


def _first_norm(xp, xs, g):
    n_p = P_ROWS // ROW_TILE
    return pl.pallas_call(
        _first_norm_kernel,
        grid=(ROWS // ROW_TILE,),
        in_specs=[
            pl.BlockSpec((ROW_TILE, D_MODEL), lambda m: (jnp.minimum(m, n_p - 1), 0)),
            pl.BlockSpec((ROW_TILE, D_MODEL), lambda m: (0, 0)),
            pl.BlockSpec((1, D_MODEL), lambda m: (0, 0)),
        ],
        out_specs=pl.BlockSpec((ROW_TILE, D_MODEL), lambda m: (m, 0)),
        out_shape=jax.ShapeDtypeStruct((ROWS, D_MODEL), _bf16),
        compiler_params=_cparams(1),
        name="first_norm",
    )(xp, xs, g)


def _in_proj_kernel(h_ref, w_ref, o_ref):
    o_ref[...] = jnp.dot(h_ref[...], w_ref[...], preferred_element_type=_f32)


def _in_proj(h, w):
    n = w.shape[1]
    tn = MM_COL_TILE if n % MM_COL_TILE == 0 else n // 2
    return pl.pallas_call(
        _in_proj_kernel,
        grid=(n // tn, ROWS // MM_ROW_TILE),
        in_specs=[
            pl.BlockSpec((MM_ROW_TILE, D_MODEL), lambda j, m: (m, 0)),
            pl.BlockSpec((D_MODEL, tn), lambda j, m: (0, j)),
        ],
        out_specs=pl.BlockSpec((MM_ROW_TILE, tn), lambda j, m: (m, j)),
        out_shape=jax.ShapeDtypeStruct((ROWS, n), _f32),
        compiler_params=_cparams(2),
        name="in_proj",
    )(h, w)


def _out_proj_kernel(a_ref, w_ref, x_ref, g_ref, xn_ref, h_ref):
    xn = x_ref[...] + jnp.dot(a_ref[...], w_ref[...], preferred_element_type=_f32)
    xn_ref[...] = xn
    h_ref[...] = _rms_scale(xn, g_ref[...]).astype(h_ref.dtype)


def _out_proj_first_kernel(a_ref, w_ref, xp_ref, xs_ref, g_ref, xn_ref, h_ref):
    m = pl.program_id(0)
    x = jnp.where(m < P_ROWS // ROW_TILE, xp_ref[...], xs_ref[...])
    xn = x + jnp.dot(a_ref[...], w_ref[...], preferred_element_type=_f32)
    xn_ref[...] = xn
    h_ref[...] = _rms_scale(xn, g_ref[...]).astype(h_ref.dtype)


def _out_proj_last_kernel(a_ref, w_ref, x_ref, g_ref, yp_ref, ys_ref):
    m = pl.program_id(0)
    xn = x_ref[...] + jnp.dot(a_ref[...], w_ref[...], preferred_element_type=_f32)
    y = _rms_scale(xn, g_ref[...])

    @pl.when(m < P_ROWS // ROW_TILE)
    def _():
        yp_ref[...] = y

    @pl.when(m >= P_ROWS // ROW_TILE)
    def _():
        ys_ref[...] = y


def _row_spec(width):
    return pl.BlockSpec((ROW_TILE, width), lambda m: (m, 0))


def _out_proj(a, w, x, g):
    return pl.pallas_call(
        _out_proj_kernel,
        grid=(ROWS // ROW_TILE,),
        in_specs=[_row_spec(D_MODEL), pl.BlockSpec((D_MODEL, D_MODEL), lambda m: (0, 0)),
                  _row_spec(D_MODEL), pl.BlockSpec((1, D_MODEL), lambda m: (0, 0))],
        out_specs=[_row_spec(D_MODEL), _row_spec(D_MODEL)],
        out_shape=[jax.ShapeDtypeStruct((ROWS, D_MODEL), _f32), jax.ShapeDtypeStruct((ROWS, D_MODEL), _bf16)],
        compiler_params=_cparams(1),
        name="out_proj",
    )(a, w, x, g)


def _out_proj_first(a, w, xp, xs, g):
    n_p = P_ROWS // ROW_TILE
    return pl.pallas_call(
        _out_proj_first_kernel,
        grid=(ROWS // ROW_TILE,),
        in_specs=[_row_spec(D_MODEL), pl.BlockSpec((D_MODEL, D_MODEL), lambda m: (0, 0)),
                  pl.BlockSpec((ROW_TILE, D_MODEL), lambda m: (jnp.minimum(m, n_p - 1), 0)),
                  pl.BlockSpec((ROW_TILE, D_MODEL), lambda m: (0, 0)),
                  pl.BlockSpec((1, D_MODEL), lambda m: (0, 0))],
        out_specs=[_row_spec(D_MODEL), _row_spec(D_MODEL)],
        out_shape=[jax.ShapeDtypeStruct((ROWS, D_MODEL), _f32), jax.ShapeDtypeStruct((ROWS, D_MODEL), _bf16)],
        compiler_params=_cparams(1),
        name="out_proj_first",
    )(a, w, xp, xs, g)


def _out_proj_last(a, w, x, g):
    n_p = P_ROWS // ROW_TILE
    return pl.pallas_call(
        _out_proj_last_kernel,
        grid=(ROWS // ROW_TILE,),
        in_specs=[_row_spec(D_MODEL), pl.BlockSpec((D_MODEL, D_MODEL), lambda m: (0, 0)),
                  _row_spec(D_MODEL), pl.BlockSpec((1, D_MODEL), lambda m: (0, 0))],
        out_specs=[pl.BlockSpec((ROW_TILE, D_MODEL), lambda m: (jnp.minimum(m, n_p - 1), 0)),
                   pl.BlockSpec((ROW_TILE, D_MODEL), lambda m: (0, 0))],
        out_shape=[jax.ShapeDtypeStruct((P_ROWS, D_MODEL), _f32), jax.ShapeDtypeStruct((S_ROWS, D_MODEL), _f32)],
        compiler_params=_cparams(1),
        name="out_proj_last",
    )(a, w, x, g)


def _rope_tables(pos):
    half = ROT_DIM // 2
    inv = ROPE_THETA ** (-jnp.arange(half, dtype=_f32) * 2.0 / ROT_DIM)
    ang = pos.astype(_f32)[:, None] * inv[None, :]
    cos, sin = jnp.cos(ang), jnp.sin(ang)
    t = pos.shape[0]
    rest = A_HEAD_DIM - ROT_DIM
    c_head = jnp.concatenate([cos, cos, jnp.ones((t, rest), _f32)], axis=1)
    s1_head = jnp.concatenate([-sin, jnp.zeros((t, half + rest), _f32)], axis=1)
    s2_head = jnp.concatenate([jnp.zeros((t, half), _f32), sin, jnp.zeros((t, rest), _f32)], axis=1)
    rep = LANES // A_HEAD_DIM
    return jnp.stack([jnp.tile(c_head, (1, rep)), jnp.tile(s1_head, (1, rep)), jnp.tile(s2_head, (1, rep))])


def _rope(x, tabs):
    half = ROT_DIM // 2
    outs = []
    for c in range(x.shape[1] // LANES):
        xc = x[:, c * LANES:(c + 1) * LANES]
        outs.append(xc * tabs[0] + pltpu.roll(xc, LANES - half, 1) * tabs[1] + pltpu.roll(xc, half, 1) * tabs[2])
    return outs


def _lane_halves(xc, head_in_low_half):
    lane = lax.broadcasted_iota(jnp.int32, xc.shape, 1)
    low = lane < A_HEAD_DIM
    sw = pltpu.roll(xc, A_HEAD_DIM, 1)
    if head_in_low_half:
        return jnp.where(low, xc, 0.0), jnp.where(low, 0.0, sw)
    return jnp.where(low, sw, 0.0), jnp.where(low, 0.0, xc)


def _attn_core(q_cols, kk, vv, bias, sinks_ref, z_cols):
    nt = (((1,), (1,)), ((), ()))
    cols_per_kv = A_GROUP * A_HEAD_DIM // LANES
    tq = q_cols[0].shape[0]
    lane = lax.broadcasted_iota(jnp.int32, (tq, LANES), 1)
    low = lane < A_HEAD_DIM
    out = []
    for j in range(A_KV_HEADS):
        kc = kk[:, (j // 2) * LANES:(j // 2 + 1) * LANES]
        vc = vv[:, (j // 2) * LANES:(j // 2 + 1) * LANES]
        k_lo, k_hi = _lane_halves(kc, j % 2 == 0)
        v_lo, v_hi = _lane_halves(vc, j % 2 == 0)
        qs = jnp.concatenate(q_cols[j * cols_per_kv:(j + 1) * cols_per_kv], axis=0).astype(_bf16)
        s_lo = lax.dot_general(qs, k_lo.astype(_bf16), nt, preferred_element_type=_f32)
        s_hi = lax.dot_general(qs, k_hi.astype(_bf16), nt, preferred_element_type=_f32)
        p_lo, p_hi, r_lo, r_hi = [], [], [], []
        for c in range(cols_per_kv):
            for s_all, ps, rs, odd in ((s_lo, p_lo, r_lo, 0), (s_hi, p_hi, r_hi, 1)):
                sink = sinks_ref[j * A_GROUP + 2 * c + odd]
                s = s_all[c * tq:(c + 1) * tq] + bias
                m = jnp.maximum(jnp.max(s, axis=-1, keepdims=True), sink)
                p = jnp.exp(s - m)
                den = jnp.sum(p, axis=-1, keepdims=True) + jnp.exp(sink - m)
                ps.append(p)
                rs.append(1.0 / den)
        o = (jnp.dot(jnp.concatenate(p_lo, axis=0).astype(_bf16), v_lo.astype(_bf16), preferred_element_type=_f32)
             + jnp.dot(jnp.concatenate(p_hi, axis=0).astype(_bf16), v_hi.astype(_bf16), preferred_element_type=_f32))
        for c in range(cols_per_kv):
            oc = o[c * tq:(c + 1) * tq] * jnp.where(low, r_lo[c], r_hi[c])
            out.append(_silu(z_cols[j * cols_per_kv + c]) * oc)
    return out


def _lane_chunks(ref, start, width, rows=slice(None)):
    return [ref[rows, start + c * LANES:start + (c + 1) * LANES] for c in range(width // LANES)]


def _attn_prompt_kernel(sinks_ref, proj_ref, tab_ref, bias_ref, g_ref, ko_ref, vo_ref, kprev, vprev):
    i = pl.program_id(1)

    @pl.when(i == 0)
    def _():
        kprev[...] = jnp.zeros_like(kprev)
        vprev[...] = jnp.zeros_like(vprev)

    tabs = tab_ref[...]
    q_cols = [c * (A_HEAD_DIM ** -0.5) for c in _rope(proj_ref[:, :A_Q_DIM], tabs)]
    k_new = jnp.concatenate(_rope(proj_ref[:, A_Q_DIM:A_Q_DIM + A_KV_DIM], tabs), axis=1)
    v_new = proj_ref[:, A_Q_DIM + A_KV_DIM:A_Q_DIM + 2 * A_KV_DIM]
    ko_ref[0] = k_new
    vo_ref[0] = v_new
    kk = jnp.concatenate([kprev[...], k_new], axis=0)
    vv = jnp.concatenate([vprev[...], v_new], axis=0)
    z_cols = _lane_chunks(proj_ref, A_Q_DIM + 2 * A_KV_DIM, A_Q_DIM)
    g_cols = _attn_core(q_cols, kk, vv, bias_ref[0], sinks_ref, z_cols)
    for c, gc in enumerate(g_cols):
        g_ref[:, c * LANES:(c + 1) * LANES] = gc.astype(g_ref.dtype)
    kprev[...] = k_new
    vprev[...] = v_new


def _prompt_bias():
    r = jnp.arange(WINDOW)[:, None]
    c = jnp.arange(2 * WINDOW)[None, :]
    band = (c > r) & (c <= r + WINDOW)
    first = band & (c >= WINDOW)
    return jnp.where(jnp.stack([first, band]), 0.0, NEG_BIG).astype(_f32)


def _attn_prompt(proj, sinks, tabs):
    nb = SEQ // WINDOW
    return pl.pallas_call(
        _attn_prompt_kernel,
        grid=(BATCH, nb),
        in_specs=[
            pl.BlockSpec(memory_space=pltpu.SMEM),
            pl.BlockSpec((WINDOW, A_IN_DIM), lambda b, i: (b * nb + i, 0)),
            pl.BlockSpec((3, WINDOW, LANES), lambda b, i: (0, i, 0)),
            pl.BlockSpec((1, WINDOW, 2 * WINDOW), lambda b, i: (jnp.minimum(i, 1), 0, 0)),
        ],
        out_specs=[
            pl.BlockSpec((WINDOW, A_Q_DIM), lambda b, i: (b * nb + i, 0)),
            pl.BlockSpec((1, WINDOW, A_KV_DIM), lambda b, i: (b, 0, 0)),
            pl.BlockSpec((1, WINDOW, A_KV_DIM), lambda b, i: (b, 0, 0)),
        ],
        out_shape=[
            jax.ShapeDtypeStruct((ROWS, A_Q_DIM), _bf16),
            jax.ShapeDtypeStruct((BATCH, WINDOW, A_KV_DIM), _f32),
            jax.ShapeDtypeStruct((BATCH, WINDOW, A_KV_DIM), _f32),
        ],
        scratch_shapes=[pltpu.VMEM((WINDOW, A_KV_DIM), _f32), pltpu.VMEM((WINDOW, A_KV_DIM), _f32)],
        compiler_params=_cparams(2),
        name="attn_prompt",
    )(sinks, proj, tabs, _prompt_bias())


def _attn_sample_kernel(sinks_ref, proj_ref, tab_ref, bias_ref, ck_ref, cv_ref, gin_ref, g_ref, ko_ref, vo_ref,
                        kk_ref, vv_ref):
    del gin_ref
    tabs = tab_ref[...]
    q_all = [c * (A_HEAD_DIM ** -0.5) for c in _rope(proj_ref[:, :A_Q_DIM], tabs)]
    k_all = jnp.concatenate(_rope(proj_ref[:, A_Q_DIM:A_Q_DIM + A_KV_DIM], tabs), axis=1)
    v_all = proj_ref[:, A_Q_DIM + A_KV_DIM:A_Q_DIM + 2 * A_KV_DIM]
    keep = WINDOW - DEC_SEQ
    g_parts = []
    for s in range(SAMPLE_SEQS):
        rows = slice(s * DEC_SEQ, (s + 1) * DEC_SEQ)
        k_new, v_new = k_all[rows], v_all[rows]
        ko_ref[s, :keep] = ck_ref[s, DEC_SEQ:]
        ko_ref[s, keep:] = k_new
        vo_ref[s, :keep] = cv_ref[s, DEC_SEQ:]
        vo_ref[s, keep:] = v_new
        for ref, cache, new in ((kk_ref, ck_ref, k_new), (vv_ref, cv_ref, v_new)):
            ref[:WINDOW] = cache[s]
            ref[WINDOW:WINDOW + DEC_SEQ] = new
            ref[WINDOW + DEC_SEQ:] = jnp.zeros((WINDOW - DEC_SEQ, A_KV_DIM), _f32)
        z_cols = _lane_chunks(proj_ref, A_Q_DIM + 2 * A_KV_DIM, A_Q_DIM, rows)
        g_parts.append(_attn_core([q[rows] for q in q_all], kk_ref[...], vv_ref[...], bias_ref[...],
                                  sinks_ref, z_cols))
    for c in range(A_Q_DIM // LANES):
        g_ref[:, c * LANES:(c + 1) * LANES] = jnp.concatenate([p[c] for p in g_parts], axis=0).astype(g_ref.dtype)


def _sample_bias():
    t = jnp.arange(DEC_SEQ)[:, None]
    c = jnp.arange(2 * WINDOW)[None, :]
    ok = (c > t) & (c <= t + WINDOW)
    return jnp.where(ok, 0.0, NEG_BIG).astype(_f32)


def _attn_sample(proj, sinks, tabs, cache_k, cache_v, g):
    srows = SAMPLE_SEQS * DEC_SEQ
    row0 = P_ROWS // srows
    return pl.pallas_call(
        _attn_sample_kernel,
        grid=(DEC_BATCH // SAMPLE_SEQS,),
        in_specs=[
            pl.BlockSpec(memory_space=pltpu.SMEM),
            pl.BlockSpec((srows, A_IN_DIM), lambda b: (row0 + b, 0)),
            pl.BlockSpec((3, srows, LANES), lambda b: (0, 0, 0)),
            pl.BlockSpec((DEC_SEQ, 2 * WINDOW), lambda b: (0, 0)),
            pl.BlockSpec((SAMPLE_SEQS, WINDOW, A_KV_DIM), lambda b: (b, 0, 0)),
            pl.BlockSpec((SAMPLE_SEQS, WINDOW, A_KV_DIM), lambda b: (b, 0, 0)),
            pl.BlockSpec(memory_space=pl.ANY),
        ],
        out_specs=[
            pl.BlockSpec((srows, A_Q_DIM), lambda b: (row0 + b, 0)),
            pl.BlockSpec((SAMPLE_SEQS, WINDOW, A_KV_DIM), lambda b: (b, 0, 0)),
            pl.BlockSpec((SAMPLE_SEQS, WINDOW, A_KV_DIM), lambda b: (b, 0, 0)),
        ],
        out_shape=[
            jax.ShapeDtypeStruct((ROWS, A_Q_DIM), _bf16),
            jax.ShapeDtypeStruct((DEC_BATCH, WINDOW, A_KV_DIM), _f32),
            jax.ShapeDtypeStruct((DEC_BATCH, WINDOW, A_KV_DIM), _f32),
        ],
        scratch_shapes=[pltpu.VMEM((2 * WINDOW, A_KV_DIM), _f32), pltpu.VMEM((2 * WINDOW, A_KV_DIM), _f32)],
        input_output_aliases={6: 0},
        compiler_params=_cparams(1),
        name="attn_sample",
    )(sinks, proj, tabs, _sample_bias(), cache_k, cache_v, g)


CONV_COLS = 256


def _conv_taps(u, u1, u2, w_ref):
    return w_ref[0:1, :] * u2 + w_ref[1:2, :] * u1 + w_ref[2:3, :] * u


def _conv_prompt_kernel(bg_ref, cg_ref, xi_ref, z_ref, w_ref, g_ref, tail_ref):
    u = cg_ref[...] * xi_ref[...]
    row = lax.broadcasted_iota(jnp.int32, u.shape, 0)
    u1 = jnp.where(row >= 1, pltpu.roll(u, 1, 0), 0.0)
    u2 = jnp.where(row >= 2, pltpu.roll(u, 2, 0), 0.0)
    conv = _conv_taps(u, u1, u2, w_ref)
    g_ref[...] = (_silu(z_ref[...]) * bg_ref[...] * conv).astype(g_ref.dtype)
    tail_ref[0] = u[SEQ - SUBLANES:]


def _conv_prompt(proj, conv_w):
    ncb = D_MODEL // CONV_COLS

    def seg(k):
        return pl.BlockSpec((SEQ, CONV_COLS), lambda b, c: (b, k * ncb + c))

    return pl.pallas_call(
        _conv_prompt_kernel,
        grid=(BATCH, ncb),
        in_specs=[seg(0), seg(1), seg(2), seg(3), pl.BlockSpec((CONV_WIDTH, CONV_COLS), lambda b, c: (0, c))],
        out_specs=[pl.BlockSpec((SEQ, CONV_COLS), lambda b, c: (b, c)),
                   pl.BlockSpec((1, SUBLANES, CONV_COLS), lambda b, c: (b, 0, c))],
        out_shape=[jax.ShapeDtypeStruct((ROWS, D_MODEL), _bf16),
                   jax.ShapeDtypeStruct((BATCH, SUBLANES, D_MODEL), _f32)],
        compiler_params=_cparams(2),
        name="conv_prompt",
    )(proj, proj, proj, proj, conv_w)


def _conv_sample_kernel(bg_ref, cg_ref, xi_ref, z_ref, w_ref, past_ref, gin_ref, g_ref, u_ref):
    del gin_ref
    u = cg_ref[...] * xi_ref[...]
    past = past_ref[...]
    t = lax.broadcasted_iota(jnp.int32, u.shape, 0) % DEC_SEQ
    u1 = jnp.where(t >= 1, pltpu.roll(u, 1, 0), pltpu.roll(past, S_ROWS - 1, 0))
    u2 = jnp.where(t >= 2, pltpu.roll(u, 2, 0), past)
    conv = _conv_taps(u, u1, u2, w_ref)
    g_ref[...] = (_silu(z_ref[...]) * bg_ref[...] * conv).astype(g_ref.dtype)
    u_ref[...] = u


def _conv_sample(proj, conv_w, past, g):
    ncb = D_MODEL // CONV_COLS
    rb = P_ROWS // S_ROWS

    def seg(k):
        return pl.BlockSpec((S_ROWS, CONV_COLS), lambda c: (rb, k * ncb + c))

    return pl.pallas_call(
        _conv_sample_kernel,
        grid=(ncb,),
        in_specs=[seg(0), seg(1), seg(2), seg(3), pl.BlockSpec((CONV_WIDTH, CONV_COLS), lambda c: (0, c)),
                  pl.BlockSpec((S_ROWS, CONV_COLS), lambda c: (0, c)),
                  pl.BlockSpec(memory_space=pl.ANY)],
        out_specs=[pl.BlockSpec((S_ROWS, CONV_COLS), lambda c: (rb, c)),
                   pl.BlockSpec((S_ROWS, CONV_COLS), lambda c: (0, c))],
        out_shape=[jax.ShapeDtypeStruct((ROWS, D_MODEL), _bf16),
                   jax.ShapeDtypeStruct((S_ROWS, D_MODEL), _f32)],
        input_output_aliases={6: 0},
        compiler_params=_cparams(1),
        name="conv_sample",
    )(proj, proj, proj, proj, conv_w, past, g)


HG_ROWS = 256
HG_HEADS = 4


def _lower_bound(logit_ref, layer):
    lg = logit_ref[...]
    e = jnp.exp(lg - jnp.max(lg, axis=0, keepdims=True))
    return jnp.sum(e[1:layer + 1], axis=0, keepdims=True) / jnp.sum(e, axis=0, keepdims=True)


def _hgrn_gates(q, fl, lb):
    f = lb + (1.0 - lb) * jax.nn.sigmoid(fl)
    return _silu(q), 1.0 - f, jnp.log(f)


def _chunk_cumsum(x, chunk):
    t = lax.broadcasted_iota(jnp.int32, x.shape, 0) % chunk
    s = 1
    while s < chunk:
        x = x + jnp.where(t >= s, pltpu.roll(x, s, 0), 0.0)
        s *= 2
    return x


def _hgrn_out(o, z, ng):
    ms = jnp.mean(o * o, axis=-1, keepdims=True)
    return _silu(z) * (o * lax.rsqrt(ms + RMS_EPS) * ng)


def _hgrn_prompt_head(lb, q, fl, v, z, ng, st):
    nt = (((1,), (1,)), ((), ()))
    tn = (((0,), (0,)), ((), ()))
    n_chunks = HG_ROWS // C_CHUNK
    qq, kk, lf = _hgrn_gates(q, fl, lb)
    b = _chunk_cumsum(lf, C_CHUNK)
    b_last = [b[(c + 1) * C_CHUNK - 1:(c + 1) * C_CHUNK] for c in range(n_chunks)]
    b_end = jnp.concatenate([jnp.broadcast_to(r, (C_CHUNK, C_KEY_DIM)) for r in b_last], axis=0)
    q_t = (qq * jnp.exp(b)).astype(_bf16)
    k_intra = (kk * jnp.exp(-b)).astype(_bf16)
    k_state = (kk * jnp.exp(b_end - b)).astype(_bf16)
    vb = v.astype(_bf16)
    row = lax.broadcasted_iota(jnp.int32, (HG_ROWS, HG_ROWS), 0)
    col = lax.broadcasted_iota(jnp.int32, (HG_ROWS, HG_ROWS), 1)
    keep = (col <= row) & (col >= row - row % C_CHUNK)
    att = jnp.where(keep, lax.dot_general(q_t, k_intra, nt, preferred_element_type=_f32), 0.0)
    o_intra = jnp.dot(att.astype(_bf16), vb, preferred_element_type=_f32)
    sl = [slice(c * C_CHUNK, (c + 1) * C_CHUNK) for c in range(n_chunks)]
    upd = [lax.dot_general(vb[sl[c]], k_state[sl[c]], tn, preferred_element_type=_f32) for c in range(n_chunks)]
    starts = []
    for c in range(n_chunks):
        starts.append(st.astype(_bf16))
        st = st * jnp.exp(b_last[c]) + upd[c]
    o_inter = jnp.concatenate([lax.dot_general(q_t[sl[c]], starts[c], nt, preferred_element_type=_f32)
                               for c in range(n_chunks)], axis=0)
    return _hgrn_out(o_inter + o_intra, z, ng), st


def _hgrn_prompt_kernel(layer, q_ref, f_ref, i_ref, z_ref, lg_ref, ng_ref, g_ref, so_ref, st_ref):
    tt = pl.program_id(2)

    @pl.when(tt == 0)
    def _():
        st_ref[...] = jnp.zeros_like(st_ref)

    lb_all = _lower_bound(lg_ref, layer)
    for hh in range(HG_HEADS):
        hs = slice(hh * LANES, (hh + 1) * LANES)
        g, st = _hgrn_prompt_head(lb_all[:, hs], q_ref[:, hs], f_ref[:, hs], i_ref[:, hs], z_ref[:, hs],
                                  ng_ref[...], st_ref[hh])
        g_ref[:, hs] = g.astype(g_ref.dtype)
        st_ref[hh] = st

    @pl.when(tt == pl.num_programs(2) - 1)
    def _():
        for hh in range(HG_HEADS):
            so_ref[0, hh] = st_ref[hh].T


def _hgrn_prompt(proj, lb_logits, norm_g, layer):
    nt = SEQ // HG_ROWS
    nhb = C_HEADS // HG_HEADS
    cols = HG_HEADS * LANES

    def seg(k):
        return pl.BlockSpec((HG_ROWS, cols), lambda b, h, t: (b * nt + t, k * nhb + h))

    return pl.pallas_call(
        functools.partial(_hgrn_prompt_kernel, layer),
        grid=(BATCH, nhb, nt),
        in_specs=[seg(0), seg(1), seg(2), seg(3),
                  pl.BlockSpec((DEPTH, cols), lambda b, h, t: (0, h)),
                  pl.BlockSpec((1, C_VAL_DIM), lambda b, h, t: (0, 0))],
        out_specs=[pl.BlockSpec((HG_ROWS, cols), lambda b, h, t: (b * nt + t, h)),
                   pl.BlockSpec((1, HG_HEADS, C_KEY_DIM, C_VAL_DIM), lambda b, h, t: (b, h, 0, 0))],
        out_shape=[jax.ShapeDtypeStruct((ROWS, D_MODEL), _bf16),
                   jax.ShapeDtypeStruct((BATCH, C_HEADS, C_KEY_DIM, C_VAL_DIM), _f32)],
        scratch_shapes=[pltpu.VMEM((HG_HEADS, C_VAL_DIM, C_KEY_DIM), _f32)],
        compiler_params=_cparams(3),
        name="hgrn_prompt",
    )(proj, proj, proj, proj, lb_logits, norm_g)


def _hgrn_sample_kernel(layer, proj_ref, lg_ref, ng_ref, s_ref, gin_ref, g_ref, so_ref):
    del gin_ref
    nt = (((1,), (1,)), ((), ()))
    tn = (((0,), (0,)), ((), ()))
    row = lax.broadcasted_iota(jnp.int32, (DEC_SEQ, DEC_SEQ), 0)
    col = lax.broadcasted_iota(jnp.int32, (DEC_SEQ, DEC_SEQ), 1)
    causal = row >= col
    eye = (lax.broadcasted_iota(jnp.int32, (C_KEY_DIM, C_KEY_DIM), 0)
           == lax.broadcasted_iota(jnp.int32, (C_KEY_DIM, C_KEY_DIM), 1))
    for h in range(C_HEADS):
        hs = slice(h * LANES, (h + 1) * LANES)
        lb = _lower_bound(lg_ref.at[:, hs], layer)
        qq, kk, lf = _hgrn_gates(proj_ref[:, h * LANES:(h + 1) * LANES],
                                 proj_ref[:, C_F_DIM + h * LANES:C_F_DIM + (h + 1) * LANES], lb)
        v = proj_ref[:, 2 * C_F_DIM + h * LANES:2 * C_F_DIM + (h + 1) * LANES]
        z = proj_ref[:, 2 * C_F_DIM + D_MODEL + h * LANES:2 * C_F_DIM + D_MODEL + (h + 1) * LANES]
        b = _chunk_cumsum(lf, DEC_SEQ)
        q_t = qq * jnp.exp(b)
        k_intra = kk * jnp.exp(-b)
        outs = []
        for s in range(SAMPLE_SEQS):
            rows = slice(s * DEC_SEQ, (s + 1) * DEC_SEQ)
            b_last = b[(s + 1) * DEC_SEQ - 1:(s + 1) * DEC_SEQ]
            qb = q_t[rows].astype(_bf16)
            k_state = (kk[rows] * jnp.exp(b_last - b[rows])).astype(_bf16)
            vb = v[rows].astype(_bf16)
            s0 = s_ref[s, h]
            o_inter = jnp.dot(qb, s0.astype(_bf16), preferred_element_type=_f32)
            att = jnp.where(causal, lax.dot_general(qb, k_intra[rows].astype(_bf16), nt,
                                                    preferred_element_type=_f32), 0.0)
            outs.append(o_inter + jnp.dot(att.astype(_bf16), vb, preferred_element_type=_f32))
            decay = jnp.sum(jnp.where(eye, jnp.broadcast_to(jnp.exp(b_last), eye.shape), 0.0),
                            axis=1, keepdims=True)
            so_ref[s, h] = decay * s0 + lax.dot_general(k_state, vb, tn, preferred_element_type=_f32)
        g_ref[:, hs] = _hgrn_out(jnp.concatenate(outs, axis=0), z, ng_ref[...]).astype(g_ref.dtype)


def _hgrn_sample(proj, lb_logits, norm_g, state, g, layer):
    srows = SAMPLE_SEQS * DEC_SEQ
    row0 = P_ROWS // srows
    return pl.pallas_call(
        functools.partial(_hgrn_sample_kernel, layer),
        grid=(DEC_BATCH // SAMPLE_SEQS,),
        in_specs=[pl.BlockSpec((srows, C_IN_DIM), lambda b: (row0 + b, 0)),
                  pl.BlockSpec((DEPTH, C_F_DIM), lambda b: (0, 0)),
                  pl.BlockSpec((1, C_VAL_DIM), lambda b: (0, 0)),
                  pl.BlockSpec((SAMPLE_SEQS, C_HEADS, C_KEY_DIM, C_VAL_DIM), lambda b: (b, 0, 0, 0)),
                  pl.BlockSpec(memory_space=pl.ANY)],
        out_specs=[pl.BlockSpec((srows, D_MODEL), lambda b: (row0 + b, 0)),
                   pl.BlockSpec((SAMPLE_SEQS, C_HEADS, C_KEY_DIM, C_VAL_DIM), lambda b: (b, 0, 0, 0))],
        out_shape=[jax.ShapeDtypeStruct((ROWS, D_MODEL), _bf16),
                   jax.ShapeDtypeStruct((DEC_BATCH, C_HEADS, C_KEY_DIM, C_VAL_DIM), _f32)],
        input_output_aliases={4: 0},
        compiler_params=_cparams(1),
        name="hgrn_sample",
    )(proj, lb_logits, norm_g, state, g)


def kernel(x_prompt, x_sample, cache_k, cache_v, state_conv, state_hgrn, ln_g, final_g, a_w_in, a_w_out, a_sinks,
           b_w_in, b_conv_w, b_w_out, c_w_in, c_norm_g, c_w_out, c_lb_logits):
    xp = x_prompt.reshape(P_ROWS, D_MODEL)
    xs = x_sample.reshape(S_ROWS, D_MODEL)
    tabs_p = _rope_tables(jnp.arange(SEQ, dtype=jnp.int32))
    tabs_s = _rope_tables(PAST_LEN + jnp.arange(SAMPLE_SEQS * DEC_SEQ, dtype=jnp.int32) % DEC_SEQ)
    norm_gains = [ln_g[li].reshape(1, D_MODEL) for li in range(1, DEPTH)] + [final_g.reshape(1, D_MODEL)]

    h = _first_norm(xp, xs, ln_g[0].reshape(1, D_MODEL))
    x = None
    ks_p, vs_p, ks_s, vs_s, conv_p, conv_s, hg_p, hg_s = [], [], [], [], [], [], [], []
    for li in range(DEPTH):
        kind, j = li % 3, li // 3
        if kind == 0:
            proj = _in_proj(h, a_w_in[j].astype(_bf16))
            g, k_p, v_p = _attn_prompt(proj, a_sinks[j], tabs_p)
            g, k_s, v_s = _attn_sample(proj, a_sinks[j], tabs_s,
                                       cache_k[j].reshape(DEC_BATCH, WINDOW, A_KV_DIM),
                                       cache_v[j].reshape(DEC_BATCH, WINDOW, A_KV_DIM), g)
            ks_p.append(k_p), vs_p.append(v_p), ks_s.append(k_s), vs_s.append(v_s)
            w_out = a_w_out[j]
        elif kind == 1:
            proj = _in_proj(h, b_w_in[j].astype(_bf16))
            g, tail = _conv_prompt(proj, b_conv_w[j])
            past = jnp.pad(state_conv[j], ((0, 0), (0, DEC_SEQ - (CONV_WIDTH - 1)), (0, 0))).reshape(S_ROWS, D_MODEL)
            g, u_s = _conv_sample(proj, b_conv_w[j], past, g)
            conv_p.append(tail[:, SUBLANES - (CONV_WIDTH - 1):])
            conv_s.append(u_s.reshape(DEC_BATCH, DEC_SEQ, D_MODEL)[:, DEC_SEQ - (CONV_WIDTH - 1):])
            w_out = b_w_out[j]
        else:
            proj = _in_proj(h, c_w_in[j].astype(_bf16))
            ng = c_norm_g[j].reshape(1, C_VAL_DIM)
            g, s_p = _hgrn_prompt(proj, c_lb_logits, ng, li)
            g, s_s = _hgrn_sample(proj, c_lb_logits, ng, state_hgrn[j], g, li)
            hg_p.append(s_p), hg_s.append(s_s)
            w_out = c_w_out[j]
        w_out = w_out.astype(_bf16)
        if li == 0:
            x, h = _out_proj_first(g, w_out, xp, xs, norm_gains[li])
        elif li < DEPTH - 1:
            x, h = _out_proj(g, w_out, x, norm_gains[li])
        else:
            y_p, y_s = _out_proj_last(g, w_out, x, norm_gains[li])

    kv_p = lambda xs_: jnp.stack(xs_).reshape(len(xs_), BATCH, WINDOW, A_KV_HEADS, A_HEAD_DIM)
    kv_s = lambda xs_: jnp.stack(xs_).reshape(len(xs_), DEC_BATCH, WINDOW, A_KV_HEADS, A_HEAD_DIM)
    return (y_p.reshape(BATCH, SEQ, D_MODEL), y_s.reshape(DEC_BATCH, DEC_SEQ, D_MODEL),
            kv_p(ks_p), kv_p(vs_p), kv_s(ks_s), kv_s(vs_s),
            jnp.stack(conv_p), jnp.stack(conv_s), jnp.stack(hg_p), jnp.stack(hg_s))
```

```python
import functools

import jax
import jax.numpy as jnp
from jax import lax
from jax.experimental import pallas as pl
from jax.experimental.pallas import tpu as pltpu

D_MODEL = 2048
BATCH = 4
SEQ = 2048
DEPTH = 4
DEC_BATCH = 32
DEC_SEQ = 8
PAST_LEN = 16384
RMS_EPS = 1e-6

A_HEADS = 32
A_KV_HEADS = 4
A_HEAD_DIM = 64
A_GROUP = A_HEADS // A_KV_HEADS
A_Q_DIM = A_HEADS * A_HEAD_DIM
A_KV_DIM = A_KV_HEADS * A_HEAD_DIM
A_IN_DIM = 2 * A_Q_DIM + 2 * A_KV_DIM
WINDOW = 128
ROT_DIM = A_HEAD_DIM // 4
ROPE_THETA = 500000.0

CONV_WIDTH = 3
B_IN_DIM = 4 * D_MODEL

C_HEADS = 16
C_KEY_DIM = 128
C_VAL_DIM = D_MODEL // C_HEADS
C_F_DIM = C_HEADS * C_KEY_DIM
C_IN_DIM = 2 * C_F_DIM + 2 * D_MODEL
C_CHUNK = 32

LANES = 128
SUBLANES = 8
MXU_COLS = 256
P_ROWS = BATCH * SEQ
S_ROWS = DEC_BATCH * DEC_SEQ
ROWS = P_ROWS + S_ROWS
ROW_TILE = 256
SAMPLE_SEQS = 2
MM_ROW_TILE = ROWS // 8
MM_COL_TILE = 1024
ROPE_ROW_PARTS = 3
NEG_BIG = -1e30
VMEM_LIMIT = 56 * 1024 * 1024

_f32 = jnp.float32
_bf16 = jnp.bfloat16


def _cparams(n_axes):
    return pltpu.CompilerParams(dimension_semantics=("arbitrary",) * n_axes, vmem_limit_bytes=VMEM_LIMIT)


def _silu(x):
    return x * jax.nn.sigmoid(x)


def _rms_scale(x, g):
    ms = jnp.mean(x * x, axis=-1, keepdims=True)
    return x * lax.rsqrt(ms + RMS_EPS) * g


def _first_norm_kernel(xp_ref, xs_ref, g_ref, h_ref):
    m = pl.program_id(0)
    x = jnp.where(m < P_ROWS // ROW_TILE, xp_ref[...], xs_ref[...])
    h_ref[...] = _rms_scale(x, g_ref[...]).astype(h_ref.dtype)


def _first_norm(xp, xs, g):
    n_p = P_ROWS // ROW_TILE
    return pl.pallas_call(
        _first_norm_kernel,
        grid=(ROWS // ROW_TILE,),
        in_specs=[
            pl.BlockSpec((ROW_TILE, D_MODEL), lambda m: (jnp.minimum(m, n_p - 1), 0)),
            pl.BlockSpec((ROW_TILE, D_MODEL), lambda m: (0, 0)),
            pl.BlockSpec((1, D_MODEL), lambda m: (0, 0)),
        ],
        out_specs=pl.BlockSpec((ROW_TILE, D_MODEL), lambda m: (m, 0)),
        out_shape=jax.ShapeDtypeStruct((ROWS, D_MODEL), _bf16),
        compiler_params=_cparams(1),
        name="first_norm",
    )(xp, xs, g)


def _cast_weight_once(w_ref, wb_ref):
    @pl.when(pl.program_id(1) == 0)
    def _():
        wb_ref[...] = w_ref[...].astype(_bf16)


def _in_proj_kernel(h_ref, w_ref, o_ref, wb_ref):
    _cast_weight_once(w_ref, wb_ref)
    o_ref[...] = jnp.dot(h_ref[...], wb_ref[...], preferred_element_type=_f32)


def _in_proj_rope_kernel(tn, h_ref, w_ref, tab_ref, o_ref, wb_ref):
    _cast_weight_once(w_ref, wb_ref)
    half = ROT_DIM // 2
    chunks = tn // LANES
    part = MM_ROW_TILE // ROPE_ROW_PARTS
    for r in range(ROPE_ROW_PARTS):
        rows = slice(r * part, (r + 1) * part)
        y = jnp.dot(h_ref[rows, :], wb_ref[...], preferred_element_type=_f32)
        tabs = [tab_ref[k, rows, :] for k in range(3)]
        for c in range(chunks):
            yc = y[:, c * LANES:(c + 1) * LANES]
            is_query = pl.program_id(0) * chunks + c < A_Q_DIM // LANES
            scale = jnp.where(is_query, A_HEAD_DIM ** -0.5, 1.0)
            rc = yc * tabs[0] + pltpu.roll(yc, LANES - half, 1) * tabs[1] + pltpu.roll(yc, half, 1) * tabs[2]
            o_ref[rows, c * LANES:(c + 1) * LANES] = rc * scale


def _in_proj(h, w, wj, n_out, tn, col0=0, rope_tabs=None):
    cb0 = col0 // tn
    in_specs = [pl.BlockSpec((MM_ROW_TILE, D_MODEL), lambda j, m: (m, 0)),
                pl.BlockSpec((None, D_MODEL, tn), lambda j, m: (wj, 0, cb0 + j))]
    args = [h, w]
    if rope_tabs is None:
        body = _in_proj_kernel
    else:
        body = functools.partial(_in_proj_rope_kernel, tn)
        in_specs.append(pl.BlockSpec((3, MM_ROW_TILE, LANES), lambda j, m: (0, m, 0)))
        args.append(rope_tabs)
    return pl.pallas_call(
        body,
        grid=(n_out // tn, ROWS // MM_ROW_TILE),
        in_specs=in_specs,
        out_specs=pl.BlockSpec((MM_ROW_TILE, tn), lambda j, m: (m, j)),
        out_shape=jax.ShapeDtypeStruct((ROWS, n_out), _f32),
        scratch_shapes=[pltpu.VMEM((D_MODEL, tn), _bf16)],
        compiler_params=_cparams(2),
        name="in_proj" if rope_tabs is None else "in_proj_rope",
    )(*args)


def _out_dot(a_ref, w_ref, wb_ref):
    @pl.when(pl.program_id(0) == 0)
    def _():
        wb_ref[...] = w_ref[...].astype(_bf16)

    return jnp.dot(a_ref[...], wb_ref[...], preferred_element_type=_f32)


def _out_proj_kernel(a_ref, w_ref, x_ref, g_ref, xn_ref, h_ref, wb_ref):
    xn = x_ref[...] + _out_dot(a_ref, w_ref, wb_ref)
    xn_ref[...] = xn
    h_ref[...] = _rms_scale(xn, g_ref[...]).astype(h_ref.dtype)


def _out_proj_first_kernel(a_ref, w_ref, xp_ref, xs_ref, g_ref, xn_ref, h_ref, wb_ref):
    m = pl.program_id(0)
    x = jnp.where(m < P_ROWS // ROW_TILE, xp_ref[...], xs_ref[...])
    xn = x + _out_dot(a_ref, w_ref, wb_ref)
    xn_ref[...] = xn
    h_ref[...] = _rms_scale(xn, g_ref[...]).astype(h_ref.dtype)


def _out_proj_last_kernel(a_ref, w_ref, x_ref, g_ref, yp_ref, ys_ref, wb_ref):
    m = pl.program_id(0)
    xn = x_ref[...] + _out_dot(a_ref, w_ref, wb_ref)
    y = _rms_scale(xn, g_ref[...])

    @pl.when(m < P_ROWS // ROW_TILE)
    def _():
        yp_ref[...] = y

    @pl.when(m >= P_ROWS // ROW_TILE)
    def _():
        ys_ref[...] = y


def _row_spec(width):
    return pl.BlockSpec((ROW_TILE, width), lambda m: (m, 0))


def _out_proj_call(body, name, wj, x_specs, out_specs, out_shape, args):
    gain_spec = pl.BlockSpec((1, D_MODEL), lambda m: (0, 0))
    w_spec = pl.BlockSpec((None, D_MODEL, D_MODEL), lambda m: (wj, 0, 0), pipeline_mode=pl.Buffered(1))
    return pl.pallas_call(
        body,
        grid=(ROWS // ROW_TILE,),
        in_specs=[_row_spec(D_MODEL), w_spec] + x_specs + [gain_spec],
        out_specs=out_specs,
        out_shape=out_shape,
        scratch_shapes=[pltpu.VMEM((D_MODEL, D_MODEL), _bf16)],
        compiler_params=_cparams(1),
        name=name,
    )(*args)


def _xn_h_shapes():
    return [jax.ShapeDtypeStruct((ROWS, D_MODEL), _f32), jax.ShapeDtypeStruct((ROWS, D_MODEL), _bf16)]


def _out_proj(a, w, wj, x, g):
    return _out_proj_call(_out_proj_kernel, "out_proj", wj, [_row_spec(D_MODEL)],
                          [_row_spec(D_MODEL), _row_spec(D_MODEL)], _xn_h_shapes(), (a, w, x, g))


def _out_proj_first(a, w, wj, xp, xs, g):
    n_p = P_ROWS // ROW_TILE
    x_specs = [pl.BlockSpec((ROW_TILE, D_MODEL), lambda m: (jnp.minimum(m, n_p - 1), 0)),
               pl.BlockSpec((ROW_TILE, D_MODEL), lambda m: (0, 0))]
    return _out_proj_call(_out_proj_first_kernel, "out_proj_first", wj, x_specs,
                          [_row_spec(D_MODEL), _row_spec(D_MODEL)], _xn_h_shapes(), (a, w, xp, xs, g))


def _out_proj_last(a, w, wj, x, g):
    n_p = P_ROWS // ROW_TILE
    out_specs = [pl.BlockSpec((ROW_TILE, D_MODEL), lambda m: (jnp.minimum(m, n_p - 1), 0)),
                 pl.BlockSpec((ROW_TILE, D_MODEL), lambda m: (0, 0))]
    out_shape = [jax.ShapeDtypeStruct((P_ROWS, D_MODEL), _f32), jax.ShapeDtypeStruct((S_ROWS, D_MODEL), _f32)]
    return _out_proj_call(_out_proj_last_kernel, "out_proj_last", wj, [_row_spec(D_MODEL)], out_specs, out_shape,
                          (a, w, x, g))


def _rope_tables(pos):
    half = ROT_DIM // 2
    inv = ROPE_THETA ** (-jnp.arange(half, dtype=_f32) * 2.0 / ROT_DIM)
    ang = pos.astype(_f32)[:, None] * inv[None, :]
    cos, sin = jnp.cos(ang), jnp.sin(ang)
    t = pos.shape[0]
    rest = A_HEAD_DIM - ROT_DIM
    c_head = jnp.concatenate([cos, cos, jnp.ones((t, rest), _f32)], axis=1)
    s1_head = jnp.concatenate([-sin, jnp.zeros((t, half + rest), _f32)], axis=1)
    s2_head = jnp.concatenate([jnp.zeros((t, half), _f32), sin, jnp.zeros((t, rest), _f32)], axis=1)
    rep = LANES // A_HEAD_DIM
    return jnp.stack([jnp.tile(c_head, (1, rep)), jnp.tile(s1_head, (1, rep)), jnp.tile(s2_head, (1, rep))])


def _lane_halves(xc, head_in_low_half):
    lane = lax.broadcasted_iota(jnp.int32, xc.shape, 1)
    low = lane < A_HEAD_DIM
    sw = pltpu.roll(xc, A_HEAD_DIM, 1)
    if head_in_low_half:
        return jnp.where(low, xc, 0.0), jnp.where(low, 0.0, sw)
    return jnp.where(low, sw, 0.0), jnp.where(low, 0.0, xc)


def _attn_core(q_cols, kk, vv, bias, sinks_ref, z_cols):
    nt = (((1,), (1,)), ((), ()))
    cols_per_kv = A_GROUP * A_HEAD_DIM // LANES
    tq = q_cols[0].shape[0]
    lane = lax.broadcasted_iota(jnp.int32, (tq, LANES), 1)
    low = lane < A_HEAD_DIM
    out = []
    for j in range(A_KV_HEADS):
        kc = kk[:, (j // 2) * LANES:(j // 2 + 1) * LANES]
        vc = vv[:, (j // 2) * LANES:(j // 2 + 1) * LANES]
        k_lo, k_hi = _lane_halves(kc, j % 2 == 0)
        v_lo, v_hi = _lane_halves(vc, j % 2 == 0)
        qs = jnp.concatenate(q_cols[j * cols_per_kv:(j + 1) * cols_per_kv], axis=0).astype(_bf16)
        s_lo = lax.dot_general(qs, k_lo.astype(_bf16), nt, preferred_element_type=_f32)
        s_hi = lax.dot_general(qs, k_hi.astype(_bf16), nt, preferred_element_type=_f32)
        p_lo, p_hi, r_lo, r_hi = [], [], [], []
        for c in range(cols_per_kv):
            for s_all, ps, rs, odd in ((s_lo, p_lo, r_lo, 0), (s_hi, p_hi, r_hi, 1)):
                sink = sinks_ref[j * A_GROUP + 2 * c + odd]
                s = s_all[c * tq:(c + 1) * tq] + bias
                m = jnp.maximum(jnp.max(s, axis=-1, keepdims=True), sink)
                p = jnp.exp(s - m)
                den = jnp.sum(p, axis=-1, keepdims=True) + jnp.exp(sink - m)
                ps.append(p)
                rs.append(1.0 / den)
        o = (jnp.dot(jnp.concatenate(p_lo, axis=0).astype(_bf16), v_lo.astype(_bf16), preferred_element_type=_f32)
             + jnp.dot(jnp.concatenate(p_hi, axis=0).astype(_bf16), v_hi.astype(_bf16), preferred_element_type=_f32))
        for c in range(cols_per_kv):
            oc = o[c * tq:(c + 1) * tq] * jnp.where(low, r_lo[c], r_hi[c])
            out.append(_silu(z_cols[j * cols_per_kv + c]) * oc)
    return out


def _attn_core_t(q_cols, kk, vv, bias_t, sinks_ref, z_cols):
    nt = (((1,), (1,)), ((), ()))
    cols_per_kv = A_GROUP * A_HEAD_DIM // LANES
    tq = q_cols[0].shape[0]
    sub = lax.broadcasted_iota(jnp.int32, (LANES, tq), 0)
    low = sub < A_HEAD_DIM
    zeros = jnp.zeros((A_HEAD_DIM, 2 * WINDOW), _f32)
    out = []
    for j in range(A_KV_HEADS):
        kc = kk[:, (j // 2) * LANES:(j // 2 + 1) * LANES]
        k_lo, k_hi = _lane_halves(kc, j % 2 == 0)
        qs = jnp.concatenate(q_cols[j * cols_per_kv:(j + 1) * cols_per_kv], axis=0).astype(_bf16)
        st_lo = lax.dot_general(k_lo.astype(_bf16), qs, nt, preferred_element_type=_f32)
        st_hi = lax.dot_general(k_hi.astype(_bf16), qs, nt, preferred_element_type=_f32)
        p_lo, p_hi, r_lo, r_hi = [], [], [], []
        for c in range(cols_per_kv):
            for st_all, ps, rs, odd in ((st_lo, p_lo, r_lo, 0), (st_hi, p_hi, r_hi, 1)):
                sink = sinks_ref[j * A_GROUP + 2 * c + odd]
                s = st_all[:, c * tq:(c + 1) * tq] + bias_t
                m = jnp.maximum(jnp.max(s, axis=0, keepdims=True), sink)
                p = jnp.exp(s - m)
                den = jnp.sum(p, axis=0, keepdims=True) + jnp.exp(sink - m)
                ps.append(p.astype(_bf16))
                rs.append(1.0 / den)
        pt = jnp.concatenate([jnp.concatenate(p_lo, axis=1), jnp.concatenate(p_hi, axis=1)], axis=0)
        vt = vv[:, (j // 2) * LANES:(j // 2 + 1) * LANES].T
        vj = vt[:A_HEAD_DIM] if j % 2 == 0 else vt[A_HEAD_DIM:]
        vcat = jnp.concatenate([jnp.concatenate([vj, zeros], axis=1), jnp.concatenate([zeros, vj], axis=1)], axis=0)
        ot = jnp.dot(vcat.astype(_bf16), pt, preferred_element_type=_f32)
        for c in range(cols_per_kv):
            oc = ot[:, c * tq:(c + 1) * tq] * jnp.where(low, r_lo[c], r_hi[c])
            out.append(_silu(z_cols[j * cols_per_kv + c]) * oc.T)
    return out


def _lane_chunks(ref, start, width, rows=slice(None)):
    return [ref[rows, start + c * LANES:start + (c + 1) * LANES] for c in range(width // LANES)]


def _attn_prompt_kernel(sinks_ref, qk_ref, vz_ref, bias_ref, g_ref, ko_ref, vo_ref, kprev, vprev):
    i = pl.program_id(1)

    @pl.when(i == 0)
    def _():
        kprev[...] = jnp.zeros_like(kprev)
        vprev[...] = jnp.zeros_like(vprev)

    k_new = qk_ref[:, A_Q_DIM:]
    v_new = vz_ref[:, :A_KV_DIM]
    ko_ref[0] = k_new
    vo_ref[0] = v_new
    kk = jnp.concatenate([kprev[...], k_new], axis=0)
    vv = jnp.concatenate([vprev[...], v_new], axis=0)
    g_cols = _attn_core_t(_lane_chunks(qk_ref, 0, A_Q_DIM), kk, vv, bias_ref[0], sinks_ref,
                          _lane_chunks(vz_ref, A_KV_DIM, A_Q_DIM))
    for c, gc in enumerate(g_cols):
        g_ref[:, c * LANES:(c + 1) * LANES] = gc.astype(g_ref.dtype)
    kprev[...] = k_new
    vprev[...] = v_new


def _prompt_bias_t():
    c = jnp.arange(2 * WINDOW)[:, None]
    r = jnp.arange(WINDOW)[None, :]
    band = (c > r) & (c <= r + WINDOW)
    first = band & (c >= WINDOW)
    return jnp.where(jnp.stack([first, band]), 0.0, NEG_BIG).astype(_f32)


A_HALF_DIM = A_Q_DIM + A_KV_DIM
A_COL_TILE = A_HALF_DIM // 3


def _attn_prompt(qk, vz, sinks):
    nb = SEQ // WINDOW
    return pl.pallas_call(
        _attn_prompt_kernel,
        grid=(BATCH, nb),
        in_specs=[
            pl.BlockSpec(memory_space=pltpu.SMEM),
            pl.BlockSpec((WINDOW, A_HALF_DIM), lambda b, i: (b * nb + i, 0)),
            pl.BlockSpec((WINDOW, A_HALF_DIM), lambda b, i: (b * nb + i, 0)),
            pl.BlockSpec((1, 2 * WINDOW, WINDOW), lambda b, i: (jnp.minimum(i, 1), 0, 0)),
        ],
        out_specs=[
            pl.BlockSpec((WINDOW, A_Q_DIM), lambda b, i: (b * nb + i, 0)),
            pl.BlockSpec((1, WINDOW, A_KV_DIM), lambda b, i: (b, 0, 0)),
            pl.BlockSpec((1, WINDOW, A_KV_DIM), lambda b, i: (b, 0, 0)),
        ],
        out_shape=[
            jax.ShapeDtypeStruct((ROWS, A_Q_DIM), _bf16),
            jax.ShapeDtypeStruct((BATCH, WINDOW, A_KV_DIM), _f32),
            jax.ShapeDtypeStruct((BATCH, WINDOW, A_KV_DIM), _f32),
        ],
        scratch_shapes=[pltpu.VMEM((WINDOW, A_KV_DIM), _f32), pltpu.VMEM((WINDOW, A_KV_DIM), _f32)],
        compiler_params=_cparams(2),
        name="attn_prompt",
    )(sinks, qk, vz, _prompt_bias_t())


def _attn_sample_kernel(sinks_ref, qk_ref, vz_ref, bias_ref, ck_ref, cv_ref, gin_ref, g_ref, ko_ref, vo_ref,
                        kk_ref, vv_ref):
    del gin_ref
    q_all = _lane_chunks(qk_ref, 0, A_Q_DIM)
    k_all = qk_ref[:, A_Q_DIM:]
    v_all = vz_ref[:, :A_KV_DIM]
    keep = WINDOW - DEC_SEQ
    g_parts = []
    for s in range(SAMPLE_SEQS):
        rows = slice(s * DEC_SEQ, (s + 1) * DEC_SEQ)
        k_new, v_new = k_all[rows], v_all[rows]
        ko_ref[s, :keep] = ck_ref[s, DEC_SEQ:]
        ko_ref[s, keep:] = k_new
        vo_ref[s, :keep] = cv_ref[s, DEC_SEQ:]
        vo_ref[s, keep:] = v_new
        for ref, cache, new in ((kk_ref, ck_ref, k_new), (vv_ref, cv_ref, v_new)):
            ref[:WINDOW] = cache[s]
            ref[WINDOW:WINDOW + DEC_SEQ] = new
            ref[WINDOW + DEC_SEQ:] = jnp.zeros((WINDOW - DEC_SEQ, A_KV_DIM), _f32)
        z_cols = _lane_chunks(vz_ref, A_KV_DIM, A_Q_DIM, rows)
        g_parts.append(_attn_core([q[rows] for q in q_all], kk_ref[...], vv_ref[...], bias_ref[...],
                                  sinks_ref, z_cols))
    for c in range(A_Q_DIM // LANES):
        g_ref[:, c * LANES:(c + 1) * LANES] = jnp.concatenate([p[c] for p in g_parts], axis=0).astype(g_ref.dtype)


def _sample_bias():
    t = jnp.arange(DEC_SEQ)[:, None]
    c = jnp.arange(2 * WINDOW)[None, :]
    ok = (c > t) & (c <= t + WINDOW)
    return jnp.where(ok, 0.0, NEG_BIG).astype(_f32)


def _attn_sample(qk, vz, sinks, cache_k, cache_v, g):
    srows = SAMPLE_SEQS * DEC_SEQ
    row0 = P_ROWS // srows
    return pl.pallas_call(
        _attn_sample_kernel,
        grid=(DEC_BATCH // SAMPLE_SEQS,),
        in_specs=[
            pl.BlockSpec(memory_space=pltpu.SMEM),
            pl.BlockSpec((srows, A_HALF_DIM), lambda b: (row0 + b, 0)),
            pl.BlockSpec((srows, A_HALF_DIM), lambda b: (row0 + b, 0)),
            pl.BlockSpec((DEC_SEQ, 2 * WINDOW), lambda b: (0, 0)),
            pl.BlockSpec((SAMPLE_SEQS, WINDOW, A_KV_DIM), lambda b: (b, 0, 0)),
            pl.BlockSpec((SAMPLE_SEQS, WINDOW, A_KV_DIM), lambda b: (b, 0, 0)),
            pl.BlockSpec(memory_space=pl.ANY),
        ],
        out_specs=[
            pl.BlockSpec((srows, A_Q_DIM), lambda b: (row0 + b, 0)),
            pl.BlockSpec((SAMPLE_SEQS, WINDOW, A_KV_DIM), lambda b: (b, 0, 0)),
            pl.BlockSpec((SAMPLE_SEQS, WINDOW, A_KV_DIM), lambda b: (b, 0, 0)),
        ],
        out_shape=[
            jax.ShapeDtypeStruct((ROWS, A_Q_DIM), _bf16),
            jax.ShapeDtypeStruct((DEC_BATCH, WINDOW, A_KV_DIM), _f32),
            jax.ShapeDtypeStruct((DEC_BATCH, WINDOW, A_KV_DIM), _f32),
        ],
        scratch_shapes=[pltpu.VMEM((2 * WINDOW, A_KV_DIM), _f32), pltpu.VMEM((2 * WINDOW, A_KV_DIM), _f32)],
        input_output_aliases={6: 0},
        compiler_params=_cparams(1),
        name="attn_sample",
    )(sinks, qk, vz, _sample_bias(), cache_k, cache_v, g)


CONV_COLS = 256


def _conv_taps(u, u1, u2, w_ref):
    return w_ref[0:1, :] * u2 + w_ref[1:2, :] * u1 + w_ref[2:3, :] * u


def _conv_prompt_kernel(bg_ref, cg_ref, xi_ref, z_ref, w_ref, g_ref, tail_ref):
    u = cg_ref[...] * xi_ref[...]
    row = lax.broadcasted_iota(jnp.int32, u.shape, 0)
    u1 = jnp.where(row >= 1, pltpu.roll(u, 1, 0), 0.0)
    u2 = jnp.where(row >= 2, pltpu.roll(u, 2, 0), 0.0)
    conv = _conv_taps(u, u1, u2, w_ref)
    g_ref[...] = (_silu(z_ref[...]) * bg_ref[...] * conv).astype(g_ref.dtype)
    tail_ref[0] = u[SEQ - SUBLANES:]


def _conv_prompt(proj, conv_w):
    ncb = D_MODEL // CONV_COLS

    def seg(k):
        return pl.BlockSpec((SEQ, CONV_COLS), lambda b, c: (b, k * ncb + c))

    return pl.pallas_call(
        _conv_prompt_kernel,
        grid=(BATCH, ncb),
        in_specs=[seg(0), seg(1), seg(2), seg(3), pl.BlockSpec((CONV_WIDTH, CONV_COLS), lambda b, c: (0, c))],
        out_specs=[pl.BlockSpec((SEQ, CONV_COLS), lambda b, c: (b, c)),
                   pl.BlockSpec((1, SUBLANES, CONV_COLS), lambda b, c: (b, 0, c))],
        out_shape=[jax.ShapeDtypeStruct((ROWS, D_MODEL), _bf16),
                   jax.ShapeDtypeStruct((BATCH, SUBLANES, D_MODEL), _f32)],
        compiler_params=_cparams(2),
        name="conv_prompt",
    )(proj, proj, proj, proj, conv_w)


def _conv_sample_kernel(bg_ref, cg_ref, xi_ref, z_ref, w_ref, past_ref, gin_ref, g_ref, u_ref):
    del gin_ref
    u = cg_ref[...] * xi_ref[...]
    past = past_ref[...]
    t = lax.broadcasted_iota(jnp.int32, u.shape, 0) % DEC_SEQ
    u1 = jnp.where(t >= 1, pltpu.roll(u, 1, 0), pltpu.roll(past, S_ROWS - 1, 0))
    u2 = jnp.where(t >= 2, pltpu.roll(u, 2, 0), past)
    conv = _conv_taps(u, u1, u2, w_ref)
    g_ref[...] = (_silu(z_ref[...]) * bg_ref[...] * conv).astype(g_ref.dtype)
    u_ref[...] = u


def _conv_sample(proj, conv_w, past, g):
    ncb = D_MODEL // CONV_COLS
    rb = P_ROWS // S_ROWS

    def seg(k):
        return pl.BlockSpec((S_ROWS, CONV_COLS), lambda c: (rb, k * ncb + c))

    return pl.pallas_call(
        _conv_sample_kernel,
        grid=(ncb,),
        in_specs=[seg(0), seg(1), seg(2), seg(3), pl.BlockSpec((CONV_WIDTH, CONV_COLS), lambda c: (0, c)),
                  pl.BlockSpec((S_ROWS, CONV_COLS), lambda c: (0, c)),
                  pl.BlockSpec(memory_space=pl.ANY)],
        out_specs=[pl.BlockSpec((S_ROWS, CONV_COLS), lambda c: (rb, c)),
                   pl.BlockSpec((S_ROWS, CONV_COLS), lambda c: (0, c))],
        out_shape=[jax.ShapeDtypeStruct((ROWS, D_MODEL), _bf16),
                   jax.ShapeDtypeStruct((S_ROWS, D_MODEL), _f32)],
        input_output_aliases={6: 0},
        compiler_params=_cparams(1),
        name="conv_sample",
    )(proj, proj, proj, proj, conv_w, past, g)


HG_ROWS = 256
HG_HEADS = 4


def _lower_bound(logit_ref, layer):
    lg = logit_ref[...]
    e = jnp.exp(lg - jnp.max(lg, axis=0, keepdims=True))
    return jnp.sum(e[1:layer + 1], axis=0, keepdims=True) / jnp.sum(e, axis=0, keepdims=True)


def _hgrn_gates(q, fl, lb):
    f = lb + (1.0 - lb) * jax.nn.sigmoid(fl)
    return _silu(q), 1.0 - f, jnp.log(f)


def _chunk_cumsum(x, chunk):
    t = lax.broadcasted_iota(jnp.int32, x.shape, 0) % chunk
    s = 1
    while s < chunk:
        x = x + jnp.where(t >= s, pltpu.roll(x, s, 0), 0.0)
        s *= 2
    return x


def _hgrn_out(o, z, ng):
    ms = jnp.mean(o * o, axis=-1, keepdims=True)
    return _silu(z) * (o * lax.rsqrt(ms + RMS_EPS) * ng)


def _hgrn_prompt_head(lb, q, fl, v, z, ng, st):
    nt = (((1,), (1,)), ((), ()))
    tn = (((0,), (0,)), ((), ()))
    n_chunks = HG_ROWS // C_CHUNK
    qq, kk, lf = _hgrn_gates(q, fl, lb)
    b = _chunk_cumsum(lf, C_CHUNK)
    b_last = [b[(c + 1) * C_CHUNK - 1:(c + 1) * C_CHUNK] for c in range(n_chunks)]
    b_end = jnp.concatenate([jnp.broadcast_to(r, (C_CHUNK, C_KEY_DIM)) for r in b_last], axis=0)
    q_t = (qq * jnp.exp(b)).astype(_bf16)
    k_intra = (kk * jnp.exp(-b)).astype(_bf16)
    k_state = (kk * jnp.exp(b_end - b)).astype(_bf16)
    vb = v.astype(_bf16)
    row = lax.broadcasted_iota(jnp.int32, (HG_ROWS, HG_ROWS), 0)
    col = lax.broadcasted_iota(jnp.int32, (HG_ROWS, HG_ROWS), 1)
    keep = (col <= row) & (col >= row - row % C_CHUNK)
    att = jnp.where(keep, lax.dot_general(q_t, k_intra, nt, preferred_element_type=_f32), 0.0)
    o_intra = jnp.dot(att.astype(_bf16), vb, preferred_element_type=_f32)
    sl = [slice(c * C_CHUNK, (c + 1) * C_CHUNK) for c in range(n_chunks)]
    upd = [lax.dot_general(vb[sl[c]], k_state[sl[c]], tn, preferred_element_type=_f32) for c in range(n_chunks)]
    starts = []
    for c in range(n_chunks):
        starts.append(st.astype(_bf16))
        st = st * jnp.exp(b_last[c]) + upd[c]
    o_inter = jnp.concatenate([lax.dot_general(q_t[sl[c]], starts[c], nt, preferred_element_type=_f32)
                               for c in range(n_chunks)], axis=0)
    return _hgrn_out(o_inter + o_intra, z, ng), st


def _hgrn_prompt_kernel(layer, q_ref, f_ref, i_ref, z_ref, lg_ref, ng_ref, g_ref, so_ref, st_ref):
    tt = pl.program_id(2)

    @pl.when(tt == 0)
    def _():
        st_ref[...] = jnp.zeros_like(st_ref)

    lb_all = _lower_bound(lg_ref, layer)
    for hh in range(HG_HEADS):
        hs = slice(hh * LANES, (hh + 1) * LANES)
        g, st = _hgrn_prompt_head(lb_all[:, hs], q_ref[:, hs], f_ref[:, hs], i_ref[:, hs], z_ref[:, hs],
                                  ng_ref[...], st_ref[hh])
        g_ref[:, hs] = g.astype(g_ref.dtype)
        st_ref[hh] = st

    @pl.when(tt == pl.num_programs(2) - 1)
    def _():
        for hh in range(HG_HEADS):
            so_ref[0, hh] = st_ref[hh].T


def _hgrn_prompt(proj, lb_logits, norm_g, layer):
    nt = SEQ // HG_ROWS
    nhb = C_HEADS // HG_HEADS
    cols = HG_HEADS * LANES

    def seg(k):
        return pl.BlockSpec((HG_ROWS, cols), lambda b, h, t: (b * nt + t, k * nhb + h))

    return pl.pallas_call(
        functools.partial(_hgrn_prompt_kernel, layer),
        grid=(BATCH, nhb, nt),
        in_specs=[seg(0), seg(1), seg(2), seg(3),
                  pl.BlockSpec((DEPTH, cols), lambda b, h, t: (0, h)),
                  pl.BlockSpec((1, C_VAL_DIM), lambda b, h, t: (0, 0))],
        out_specs=[pl.BlockSpec((HG_ROWS, cols), lambda b, h, t: (b * nt + t, h)),
                   pl.BlockSpec((1, HG_HEADS, C_KEY_DIM, C_VAL_DIM), lambda b, h, t: (b, h, 0, 0))],
        out_shape=[jax.ShapeDtypeStruct((ROWS, D_MODEL), _bf16),
                   jax.ShapeDtypeStruct((BATCH, C_HEADS, C_KEY_DIM, C_VAL_DIM), _f32)],
        scratch_shapes=[pltpu.VMEM((HG_HEADS, C_VAL_DIM, C_KEY_DIM), _f32)],
        compiler_params=_cparams(3),
        name="hgrn_prompt",
    )(proj, proj, proj, proj, lb_logits, norm_g)


def _hgrn_sample_kernel(layer, proj_ref, lg_ref, ng_ref, s_ref, gin_ref, g_ref, so_ref):
    del gin_ref
    nt = (((1,), (1,)), ((), ()))
    tn = (((0,), (0,)), ((), ()))
    row = lax.broadcasted_iota(jnp.int32, (DEC_SEQ, DEC_SEQ), 0)
    col = lax.broadcasted_iota(jnp.int32, (DEC_SEQ, DEC_SEQ), 1)
    causal = row >= col
    eye = (lax.broadcasted_iota(jnp.int32, (C_KEY_DIM, C_KEY_DIM), 0)
           == lax.broadcasted_iota(jnp.int32, (C_KEY_DIM, C_KEY_DIM), 1))
    for h in range(C_HEADS):
        hs = slice(h * LANES, (h + 1) * LANES)
        lb = _lower_bound(lg_ref.at[:, hs], layer)
        qq, kk, lf = _hgrn_gates(proj_ref[:, h * LANES:(h + 1) * LANES],
                                 proj_ref[:, C_F_DIM + h * LANES:C_F_DIM + (h + 1) * LANES], lb)
        v = proj_ref[:, 2 * C_F_DIM + h * LANES:2 * C_F_DIM + (h + 1) * LANES]
        z = proj_ref[:, 2 * C_F_DIM + D_MODEL + h * LANES:2 * C_F_DIM + D_MODEL + (h + 1) * LANES]
        b = _chunk_cumsum(lf, DEC_SEQ)
        q_t = qq * jnp.exp(b)
        k_intra = kk * jnp.exp(-b)
        outs = []
        for s in range(SAMPLE_SEQS):
            rows = slice(s * DEC_SEQ, (s + 1) * DEC_SEQ)
            b_last = b[(s + 1) * DEC_SEQ - 1:(s + 1) * DEC_SEQ]
            qb = q_t[rows].astype(_bf16)
            k_state = (kk[rows] * jnp.exp(b_last - b[rows])).astype(_bf16)
            vb = v[rows].astype(_bf16)
            s0 = s_ref[s, h]
            o_inter = jnp.dot(qb, s0.astype(_bf16), preferred_element_type=_f32)
            att = jnp.where(causal, lax.dot_general(qb, k_intra[rows].astype(_bf16), nt,
                                                    preferred_element_type=_f32), 0.0)
            outs.append(o_inter + jnp.dot(att.astype(_bf16), vb, preferred_element_type=_f32))
            decay = jnp.sum(jnp.where(eye, jnp.broadcast_to(jnp.exp(b_last), eye.shape), 0.0),
                            axis=1, keepdims=True)
            so_ref[s, h] = decay * s0 + lax.dot_general(k_state, vb, tn, preferred_element_type=_f32)
        g_ref[:, hs] = _hgrn_out(jnp.concatenate(outs, axis=0), z, ng_ref[...]).astype(g_ref.dtype)


def _hgrn_sample(proj, lb_logits, norm_g, state, g, layer):
    srows = SAMPLE_SEQS * DEC_SEQ
    row0 = P_ROWS // srows
    return pl.pallas_call(
        functools.partial(_hgrn_sample_kernel, layer),
        grid=(DEC_BATCH // SAMPLE_SEQS,),
        in_specs=[pl.BlockSpec((srows, C_IN_DIM), lambda b: (row0 + b, 0)),
                  pl.BlockSpec((DEPTH, C_F_DIM), lambda b: (0, 0)),
                  pl.BlockSpec((1, C_VAL_DIM), lambda b: (0, 0)),
                  pl.BlockSpec((SAMPLE_SEQS, C_HEADS, C_KEY_DIM, C_VAL_DIM), lambda b: (b, 0, 0, 0)),
                  pl.BlockSpec(memory_space=pl.ANY)],
        out_specs=[pl.BlockSpec((srows, D_MODEL), lambda b: (row0 + b, 0)),
                   pl.BlockSpec((SAMPLE_SEQS, C_HEADS, C_KEY_DIM, C_VAL_DIM), lambda b: (b, 0, 0, 0))],
        out_shape=[jax.ShapeDtypeStruct((ROWS, D_MODEL), _bf16),
                   jax.ShapeDtypeStruct((DEC_BATCH, C_HEADS, C_KEY_DIM, C_VAL_DIM), _f32)],
        input_output_aliases={4: 0},
        compiler_params=_cparams(1),
        name="hgrn_sample",
    )(proj, lb_logits, norm_g, state, g)


def kernel(x_prompt, x_sample, cache_k, cache_v, state_conv, state_hgrn, ln_g, final_g, a_w_in, a_w_out, a_sinks,
           b_w_in, b_conv_w, b_w_out, c_w_in, c_norm_g, c_w_out, c_lb_logits):
    xp = x_prompt.reshape(P_ROWS, D_MODEL)
    xs = x_sample.reshape(S_ROWS, D_MODEL)
    pos = jnp.concatenate([jnp.arange(P_ROWS, dtype=jnp.int32) % SEQ,
                           PAST_LEN + jnp.arange(S_ROWS, dtype=jnp.int32) % DEC_SEQ])
    tabs = _rope_tables(pos)
    norm_gains = [ln_g[li].reshape(1, D_MODEL) for li in range(1, DEPTH)] + [final_g.reshape(1, D_MODEL)]

    h = _first_norm(xp, xs, ln_g[0].reshape(1, D_MODEL))
    x = None
    ks_p, vs_p, ks_s, vs_s, conv_p, conv_s, hg_p, hg_s = [], [], [], [], [], [], [], []
    for li in range(DEPTH):
        kind, j = li % 3, li // 3
        if kind == 0:
            qk = _in_proj(h, a_w_in, j, A_HALF_DIM, A_COL_TILE, rope_tabs=tabs)
            vz = _in_proj(h, a_w_in, j, A_HALF_DIM, A_COL_TILE, col0=A_HALF_DIM)
            g, k_p, v_p = _attn_prompt(qk, vz, a_sinks[j])
            g, k_s, v_s = _attn_sample(qk, vz, a_sinks[j],
                                       cache_k[j].reshape(DEC_BATCH, WINDOW, A_KV_DIM),
                                       cache_v[j].reshape(DEC_BATCH, WINDOW, A_KV_DIM), g)
            ks_p.append(k_p), vs_p.append(v_p), ks_s.append(k_s), vs_s.append(v_s)
            w_out = a_w_out
        elif kind == 1:
            proj = _in_proj(h, b_w_in, j, B_IN_DIM, MM_COL_TILE)
            g, tail = _conv_prompt(proj, b_conv_w[j])
            past = jnp.pad(state_conv[j], ((0, 0), (0, DEC_SEQ - (CONV_WIDTH - 1)), (0, 0))).reshape(S_ROWS, D_MODEL)
            g, u_s = _conv_sample(proj, b_conv_w[j], past, g)
            conv_p.append(tail[:, SUBLANES - (CONV_WIDTH - 1):])
            conv_s.append(u_s.reshape(DEC_BATCH, DEC_SEQ, D_MODEL)[:, DEC_SEQ - (CONV_WIDTH - 1):])
            w_out = b_w_out
        else:
            proj = _in_proj(h, c_w_in, j, C_IN_DIM, MM_COL_TILE)
            ng = c_norm_g[j].reshape(1, C_VAL_DIM)
            g, s_p = _hgrn_prompt(proj, c_lb_logits, ng, li)
            g, s_s = _hgrn_sample(proj, c_lb_logits, ng, state_hgrn[j], g, li)
            hg_p.append(s_p), hg_s.append(s_s)
            w_out = c_w_out
        if li == 0:
            x, h = _out_proj_first(g, w_out, j, xp, xs, norm_gains[li])
        elif li < DEPTH - 1:
            x, h = _out_proj(g, w_out, j, x, norm_gains[li])
        else:
            y_p, y_s = _out_proj_last(g, w_out, j, x, norm_gains[li])

    def stacked(parts, *shape):
        arr = parts[0][None] if len(parts) == 1 else jnp.stack(parts)
        return arr.reshape(len(parts), *shape)

    kv_shape = (WINDOW, A_KV_HEADS, A_HEAD_DIM)
    return (y_p.reshape(BATCH, SEQ, D_MODEL), y_s.reshape(DEC_BATCH, DEC_SEQ, D_MODEL),
            stacked(ks_p, BATCH, *kv_shape), stacked(vs_p, BATCH, *kv_shape),
            stacked(ks_s, DEC_BATCH, *kv_shape), stacked(vs_s, DEC_BATCH, *kv_shape),
            stacked(conv_p, BATCH, CONV_WIDTH - 1, D_MODEL), stacked(conv_s, DEC_BATCH, CONV_WIDTH - 1, D_MODEL),
            stacked(hg_p, BATCH, C_HEADS, C_KEY_DIM, C_VAL_DIM),
            stacked(hg_s, DEC_BATCH, C_HEADS, C_KEY_DIM, C_VAL_DIM))
```

```python
import functools

import numpy as np

import jax
import jax.numpy as jnp
from jax import lax
from jax.experimental import pallas as pl
from jax.experimental.pallas import tpu as pltpu

D_MODEL = 2048
BATCH = 4
SEQ = 2048
DEPTH = 4
N_A_LAYERS = (DEPTH + 2) // 3
DEC_BATCH = 32
DEC_SEQ = 8
PAST_LEN = 16384
RMS_EPS = 1e-6

A_HEADS = 32
A_KV_HEADS = 4
A_HEAD_DIM = 64
A_GROUP = A_HEADS // A_KV_HEADS
A_Q_DIM = A_HEADS * A_HEAD_DIM
A_KV_DIM = A_KV_HEADS * A_HEAD_DIM
WINDOW = 128
ROT_DIM = A_HEAD_DIM // 4
ROPE_THETA = 500000.0

CONV_WIDTH = 3

C_HEADS = 16
C_KEY_DIM = 128
C_VAL_DIM = D_MODEL // C_HEADS
C_F_DIM = C_HEADS * C_KEY_DIM
C_IN_DIM = 2 * C_F_DIM + 2 * D_MODEL
C_CHUNK = 32

LANES = 128
SUBLANES = 8
MXU_COLS = 256
P_ROWS = BATCH * SEQ
S_ROWS = DEC_BATCH * DEC_SEQ
ROWS = P_ROWS + S_ROWS
ROW_TILE = 384
SAMPLE_SEQS = 4
MM_ROW_TILE = ROWS // 4
MM_COL_TILE = 1024
ROPE_ROW_PARTS = 6
NEG_BIG = -1e30
LOG2E = 1.4426950408889634
Q_SCALE = A_HEAD_DIM ** -0.5 * LOG2E
VMEM_LIMIT = 56 * 1024 * 1024

_f32 = jnp.float32
_bf16 = jnp.bfloat16


def _cparams(n_axes):
    return pltpu.CompilerParams(dimension_semantics=("arbitrary",) * n_axes, vmem_limit_bytes=VMEM_LIMIT)


def _silu(x):
    return x * jax.nn.sigmoid(x)


def _rms_scale(x, g):
    ms = jnp.mean(x * x, axis=-1, keepdims=True)
    return x * lax.rsqrt(ms + RMS_EPS) * g


NORM_ROW_TILE = 1024


def _first_norm_kernel(x_ref, g_ref, h_ref):
    h_ref[...] = _rms_scale(x_ref[...], g_ref[...]).astype(h_ref.dtype)


def _first_norm_sample_kernel(x_ref, g_ref, hin_ref, h_ref):
    del hin_ref
    h_ref[...] = _rms_scale(x_ref[...], g_ref[...]).astype(h_ref.dtype)


def _first_norm(xp, xs, g):
    gain_spec = pl.BlockSpec((1, D_MODEL), lambda m: (0, 0))
    h = pl.pallas_call(
        _first_norm_kernel,
        grid=(P_ROWS // NORM_ROW_TILE,),
        in_specs=[pl.BlockSpec((NORM_ROW_TILE, D_MODEL), lambda m: (m, 0)), gain_spec],
        out_specs=pl.BlockSpec((NORM_ROW_TILE, D_MODEL), lambda m: (m, 0)),
        out_shape=jax.ShapeDtypeStruct((ROWS, D_MODEL), _bf16),
        compiler_params=_cparams(1),
        name="first_norm",
    )(xp, g)
    return pl.pallas_call(
        _first_norm_sample_kernel,
        grid=(1,),
        in_specs=[pl.BlockSpec((S_ROWS, D_MODEL), lambda m: (0, 0)), gain_spec, pl.BlockSpec(memory_space=pl.ANY)],
        out_specs=pl.BlockSpec((S_ROWS, D_MODEL), lambda m: (P_ROWS // S_ROWS, 0)),
        out_shape=jax.ShapeDtypeStruct((ROWS, D_MODEL), _bf16),
        input_output_aliases={2: 0},
        compiler_params=_cparams(1),
        name="first_norm_sample",
    )(xs, g, h)


def _cast_weight_once(w_ref, wb_ref):
    @pl.when(pl.program_id(1) == 0)
    def _():
        wb_ref[...] = w_ref[...].astype(_bf16)


def _in_proj_kernel(h_ref, w_ref, o_ref, wb_ref):
    _cast_weight_once(w_ref, wb_ref)
    part = MM_ROW_TILE // ROPE_ROW_PARTS
    for r in range(ROPE_ROW_PARTS):
        rows = slice(r * part, (r + 1) * part)
        o_ref[rows, :] = jnp.dot(h_ref[rows, :], wb_ref[...], preferred_element_type=_f32).astype(o_ref.dtype)


def _in_proj_rope_kernel(tn, h_ref, w_ref, tab_ref, o_ref, wb_ref):
    _cast_weight_once(w_ref, wb_ref)
    half = ROT_DIM // 2
    chunks = tn // LANES
    part = MM_ROW_TILE // ROPE_ROW_PARTS
    for r in range(ROPE_ROW_PARTS):
        rows = slice(r * part, (r + 1) * part)
        y = jnp.dot(h_ref[rows, :], wb_ref[...], preferred_element_type=_f32)
        tabs = [tab_ref[k, rows, :] for k in range(3)]
        for c in range(chunks):
            yc = y[:, c * LANES:(c + 1) * LANES]
            is_query = pl.program_id(0) * chunks + c < A_Q_DIM // LANES
            scale = jnp.where(is_query, Q_SCALE, 1.0)
            rc = yc * tabs[0] + pltpu.roll(yc, LANES - half, 1) * tabs[1] + pltpu.roll(yc, half, 1) * tabs[2]
            o_ref[rows, c * LANES:(c + 1) * LANES] = (rc * scale).astype(o_ref.dtype)


def _in_proj(h, w, wj, n_out, tn, col0=0, rope_tabs=None):
    cb0 = col0 // tn
    in_specs = [pl.BlockSpec((MM_ROW_TILE, D_MODEL), lambda j, m: (m, 0)),
                pl.BlockSpec((None, D_MODEL, tn), lambda j, m: (wj, 0, cb0 + j))]
    args = [h, w]
    if rope_tabs is None:
        body = _in_proj_kernel
    else:
        body = functools.partial(_in_proj_rope_kernel, tn)
        in_specs.append(pl.BlockSpec((3, MM_ROW_TILE, LANES), lambda j, m: (0, m, 0)))
        args.append(rope_tabs)
    return pl.pallas_call(
        body,
        grid=(n_out // tn, ROWS // MM_ROW_TILE),
        in_specs=in_specs,
        out_specs=pl.BlockSpec((MM_ROW_TILE, tn), lambda j, m: (m, j)),
        out_shape=jax.ShapeDtypeStruct((ROWS, n_out), _bf16),
        scratch_shapes=[pltpu.VMEM((D_MODEL, tn), _bf16)],
        compiler_params=_cparams(2),
        name="in_proj" if rope_tabs is None else "in_proj_rope",
    )(*args)


def _out_proj_kernel(a_ref, w_ref, x_ref, g_ref, xn_ref, h_ref, wb_ref):
    @pl.when(pl.program_id(0) == 0)
    def _():
        wb_ref[...] = w_ref[...].astype(_bf16)

    xn = x_ref[...] + jnp.dot(a_ref[...], wb_ref[...], preferred_element_type=_f32)
    xn_ref[...] = xn
    h_ref[...] = _rms_scale(xn, g_ref[...]).astype(h_ref.dtype)


def _xn_h_shapes():
    return [jax.ShapeDtypeStruct((ROWS, D_MODEL), _f32), jax.ShapeDtypeStruct((ROWS, D_MODEL), _bf16)]


def _out_proj(a, w, wj, x, g):
    tile = pl.BlockSpec((ROW_TILE, D_MODEL), lambda m: (m, 0))
    return pl.pallas_call(
        _out_proj_kernel,
        grid=(ROWS // ROW_TILE,),
        in_specs=[tile,
                  pl.BlockSpec((None, D_MODEL, D_MODEL), lambda m: (wj, 0, 0), pipeline_mode=pl.Buffered(1)),
                  tile, pl.BlockSpec((1, D_MODEL), lambda m: (0, 0))],
        out_specs=[tile, tile],
        out_shape=_xn_h_shapes(),
        scratch_shapes=[pltpu.VMEM((D_MODEL, D_MODEL), _bf16)],
        compiler_params=_cparams(1),
        name="out_proj",
    )(a, w, x, g)


def _rope_tables():
    pos = np.concatenate([np.arange(P_ROWS) % SEQ, PAST_LEN + np.arange(S_ROWS) % DEC_SEQ]).astype(np.float64)
    half = ROT_DIM // 2
    inv = ROPE_THETA ** (-np.arange(half, dtype=np.float64) * 2.0 / ROT_DIM)
    ang = pos[:, None] * inv[None, :]
    cos, sin = np.cos(ang), np.sin(ang)
    rest = A_HEAD_DIM - ROT_DIM
    c_head = np.concatenate([cos, cos, np.ones((ROWS, rest))], axis=1)
    s1_head = np.concatenate([-sin, np.zeros((ROWS, half + rest))], axis=1)
    s2_head = np.concatenate([np.zeros((ROWS, half)), sin, np.zeros((ROWS, rest))], axis=1)
    rep = LANES // A_HEAD_DIM
    return np.stack([np.tile(t, (1, rep)) for t in (c_head, s1_head, s2_head)]).astype(np.float32)


def _lane_halves(xc, head_in_low_half):
    lane = lax.broadcasted_iota(jnp.int32, xc.shape, 1)
    low = lane < A_HEAD_DIM
    sw = pltpu.roll(xc, A_HEAD_DIM, 1)
    if head_in_low_half:
        return jnp.where(low, xc, 0.0), jnp.where(low, 0.0, sw)
    return jnp.where(low, sw, 0.0), jnp.where(low, 0.0, xc)


def _attn_core_batch(problems, bias, sinks_ref):
    nt = (((1,), (1,)), ((), ()))
    cols_per_kv = A_GROUP * A_HEAD_DIM // LANES
    tq = problems[0][0][0].shape[0]
    lane = lax.broadcasted_iota(jnp.int32, (tq, LANES), 1)
    low = lane < A_HEAD_DIM
    pairs = [(n, j) for n in range(len(problems)) for j in range(A_KV_HEADS)]
    scores, values = {}, {}
    for n, j in pairs:
        q_cols, kk, vv, _ = problems[n]
        kc = kk[:, (j // 2) * LANES:(j // 2 + 1) * LANES]
        vc = vv[:, (j // 2) * LANES:(j // 2 + 1) * LANES]
        k_lo, k_hi = _lane_halves(kc, j % 2 == 0)
        values[n, j] = _lane_halves(vc, j % 2 == 0)
        qs = jnp.concatenate(q_cols[j * cols_per_kv:(j + 1) * cols_per_kv], axis=0).astype(_bf16)
        scores[n, j] = (lax.dot_general(qs, k_lo.astype(_bf16), nt, preferred_element_type=_f32),
                        lax.dot_general(qs, k_hi.astype(_bf16), nt, preferred_element_type=_f32))
    probs = {}
    for n, j in pairs:
        p_lo, p_hi, r_lo, r_hi = [], [], [], []
        for c in range(cols_per_kv):
            for s_all, ps, rs, odd in ((scores[n, j][0], p_lo, r_lo, 0), (scores[n, j][1], p_hi, r_hi, 1)):
                sink = sinks_ref[j * A_GROUP + 2 * c + odd] * LOG2E
                s = s_all[c * tq:(c + 1) * tq] + bias
                m = jnp.maximum(jnp.max(s, axis=-1, keepdims=True), sink)
                p = jnp.exp2(s - m)
                den = jnp.sum(p, axis=-1, keepdims=True) + jnp.exp2(sink - m)
                ps.append(p)
                rs.append(1.0 / den)
        probs[n, j] = (jnp.concatenate(p_lo, axis=0).astype(_bf16), jnp.concatenate(p_hi, axis=0).astype(_bf16),
                       r_lo, r_hi)
    outs = [[] for _ in problems]
    for n, j in pairs:
        p_lo, p_hi, r_lo, r_hi = probs[n, j]
        v_lo, v_hi = values[n, j]
        o = (jnp.dot(p_lo, v_lo.astype(_bf16), preferred_element_type=_f32)
             + jnp.dot(p_hi, v_hi.astype(_bf16), preferred_element_type=_f32))
        z_cols = problems[n][3]
        for c in range(cols_per_kv):
            oc = o[c * tq:(c + 1) * tq] * jnp.where(low, r_lo[c], r_hi[c])
            outs[n].append(_silu(z_cols[j * cols_per_kv + c]) * oc)
    return outs


Q_COLS_PER_KV = A_GROUP * A_HEAD_DIM // LANES


def _attn_stage_scores(q_cols, kk, j):
    nt = (((1,), (1,)), ((), ()))
    k_lo, k_hi = _lane_halves(kk[:, (j // 2) * LANES:(j // 2 + 1) * LANES], j % 2 == 0)
    qs = jnp.concatenate(q_cols, axis=0).astype(_bf16)
    return (lax.dot_general(k_lo.astype(_bf16), qs, nt, preferred_element_type=_f32),
            lax.dot_general(k_hi.astype(_bf16), qs, nt, preferred_element_type=_f32))


def _attn_stage_values(scores, vv, j, bias_t, sinks_ref):
    tq = WINDOW
    zeros = jnp.zeros((A_HEAD_DIM, 2 * WINDOW), _f32)
    srow = lax.broadcasted_iota(jnp.int32, (2 * SUBLANES, 4 * WINDOW), 0)
    scol = lax.broadcasted_iota(jnp.int32, (2 * SUBLANES, 4 * WINDOW), 1)
    sum_rows = jnp.where(srow == scol // (2 * WINDOW), 1.0, 0.0)
    p_lo, p_hi, r_lo, r_hi = [], [], [], []
    for c in range(Q_COLS_PER_KV):
        for st_all, ps, rs, odd in ((scores[0], p_lo, r_lo, 0), (scores[1], p_hi, r_hi, 1)):
            sink = sinks_ref[j * A_GROUP + 2 * c + odd] * LOG2E
            s = st_all[:, c * tq:(c + 1) * tq] + bias_t
            m = jnp.maximum(jnp.max(s, axis=0, keepdims=True), sink)
            ps.append(jnp.exp2(s - m).astype(_bf16))
            rs.append(jnp.exp2(sink - m))
    pt = jnp.concatenate([jnp.concatenate(p_lo, axis=1), jnp.concatenate(p_hi, axis=1)], axis=0)
    vt = vv[:, (j // 2) * LANES:(j // 2 + 1) * LANES].T
    vj = vt[:A_HEAD_DIM] if j % 2 == 0 else vt[A_HEAD_DIM:]
    vcat = jnp.concatenate([jnp.concatenate([vj, zeros], axis=1), jnp.concatenate([zeros, vj], axis=1),
                            sum_rows], axis=0)
    return jnp.dot(vcat.astype(_bf16), pt, preferred_element_type=_f32), (r_lo, r_hi)


def _attn_stage_output(ot, sink_terms, z_cols):
    tq = WINDOW
    low = lax.broadcasted_iota(jnp.int32, (LANES, tq), 0) < A_HEAD_DIM
    out = []
    for c in range(Q_COLS_PER_KV):
        cols = slice(c * tq, (c + 1) * tq)
        den_lo = ot[LANES:LANES + 1, cols] + sink_terms[0][c]
        den_hi = ot[LANES + 1:LANES + 2, cols] + sink_terms[1][c]
        oc = ot[:LANES, cols] * jnp.where(low, 1.0 / den_lo, 1.0 / den_hi)
        out.append(_silu(z_cols[c]) * oc.T)
    return out


def _lane_chunks(ref, start, width, rows=slice(None), dtype=None):
    chunks = [ref[rows, start + c * LANES:start + (c + 1) * LANES] for c in range(width // LANES)]
    return chunks if dtype is None else [c.astype(dtype) for c in chunks]


def _attn_prompt_kernel(final, sinks_ref, qk_ref, vz_ref, bias_ref, w_ref, x_ref, gain_ref, *refs):
    out_refs, (ko_ref, vo_ref, kprev, vprev, g_ref, gdone_ref, wb_ref, y_ref) = refs[:-8], refs[-8:]
    s = pl.program_id(0)
    i = s % (SEQ // (ATTN_BLOCKS_PER_STEP * WINDOW))

    @pl.when(s == 0)
    def _():
        wb_ref[...] = w_ref[...].astype(_bf16)
        gdone_ref[...] = jnp.zeros_like(gdone_ref)

    @pl.when(i == 0)
    def _():
        kprev[...] = jnp.zeros_like(kprev)
        vprev[...] = jnp.zeros_like(vprev)

    def finish():
        xn = x_ref[...] + y_ref[...]
        if final:
            out_refs[0][...] = _rms_scale(xn, gain_ref[...])
        else:
            out_refs[0][...] = xn
            out_refs[1][...] = _rms_scale(xn, gain_ref[...]).astype(out_refs[1].dtype)

    @pl.when(s == pl.num_programs(0) - 1)
    def _():
        y_ref[...] = jnp.dot(gdone_ref[...], wb_ref[...], preferred_element_type=_f32)
        finish()

    @pl.when(s < pl.num_programs(0) - 1)
    def _():
        blocks = []
        k_prev, v_prev = kprev[...], vprev[...]
        for sub in range(ATTN_BLOCKS_PER_STEP):
            rows = slice(sub * WINDOW, (sub + 1) * WINDOW)
            k_new = qk_ref[rows, A_Q_DIM:].astype(_f32)
            v_new = vz_ref[rows, :A_KV_DIM].astype(_f32)
            bias = jnp.where(i == 0, bias_ref[0], bias_ref[1]) if sub == 0 else bias_ref[1]
            blocks.append((rows, jnp.concatenate([k_prev, k_new], axis=0), jnp.concatenate([v_prev, v_new], axis=0), bias))
            k_prev, v_prev = k_new, v_new
        ko_ref[0] = k_prev
        vo_ref[0] = v_prev
        kprev[...] = k_prev
        vprev[...] = v_prev

        items = [(sub, j) for sub in range(ATTN_BLOCKS_PER_STEP) for j in range(A_KV_HEADS)]
        assert len(items) == D_MODEL // MXU_COLS
        scores, values = {}, {}
        for t in range(len(items) + 2):
            if t < len(items):
                sub, j = items[t]
                rows, kk, _, _ = blocks[sub]
                q_cols = _lane_chunks(qk_ref, j * Q_COLS_PER_KV * LANES, Q_COLS_PER_KV * LANES, rows)
                scores[t] = _attn_stage_scores(q_cols, kk, j)
                cols = slice(t * MXU_COLS, (t + 1) * MXU_COLS)
                y_ref[:, cols] = jnp.dot(gdone_ref[...], wb_ref[:, cols], preferred_element_type=_f32)
            if 0 <= t - 1 < len(items):
                sub, j = items[t - 1]
                _, _, vv, bias = blocks[sub]
                values[t - 1] = _attn_stage_values(scores.pop(t - 1), vv, j, bias, sinks_ref)
            if 0 <= t - 2 < len(items):
                sub, j = items[t - 2]
                rows = blocks[sub][0]
                first = j * Q_COLS_PER_KV * LANES
                z_cols = _lane_chunks(vz_ref, A_KV_DIM + first, Q_COLS_PER_KV * LANES, rows, dtype=_f32)
                for c, gc in enumerate(_attn_stage_output(*values.pop(t - 2), z_cols)):
                    g_ref[rows, first + c * LANES:first + (c + 1) * LANES] = gc.astype(g_ref.dtype)
        gdone_ref[...] = g_ref[...]
        finish()


def _prompt_bias_t():
    c = jnp.arange(2 * WINDOW)[:, None]
    r = jnp.arange(WINDOW)[None, :]
    band = (c > r) & (c <= r + WINDOW)
    first = band & (c >= WINDOW)
    return jnp.where(jnp.stack([first, band]), 0.0, NEG_BIG).astype(_f32)


A_HALF_DIM = A_Q_DIM + A_KV_DIM
A_COL_TILE = A_HALF_DIM // 3


ATTN_BLOCKS_PER_STEP = 2


def _attn_prompt(qk, vz, sinks, w_out, wj, x, gain, final):
    rows = ATTN_BLOCKS_PER_STEP * WINDOW
    nb = SEQ // rows
    n_tiles = BATCH * nb
    attn_tile = lambda width: pl.BlockSpec((rows, width), lambda s: (jnp.minimum(s, n_tiles - 1), 0))
    proj_tile = lambda width: pl.BlockSpec((rows, width), lambda s: (jnp.maximum(s - 1, 0), 0))
    kv_spec = pl.BlockSpec((1, WINDOW, A_KV_DIM), lambda s: (jnp.minimum(s, n_tiles - 1) // nb, 0, 0))
    kv_shape = jax.ShapeDtypeStruct((BATCH, WINDOW, A_KV_DIM), _f32)
    if final:
        res_specs, res_shapes = [proj_tile(D_MODEL)], [jax.ShapeDtypeStruct((P_ROWS, D_MODEL), _f32)]
    else:
        res_specs, res_shapes = [proj_tile(D_MODEL), proj_tile(D_MODEL)], _xn_h_shapes()
    g_tile = pltpu.VMEM((rows, A_Q_DIM), _bf16)
    outs = pl.pallas_call(
        functools.partial(_attn_prompt_kernel, final),
        grid=(n_tiles + 1,),
        in_specs=[
            pl.BlockSpec(memory_space=pltpu.SMEM),
            attn_tile(A_HALF_DIM), attn_tile(A_HALF_DIM),
            pl.BlockSpec((2, 2 * WINDOW, WINDOW), lambda s: (0, 0, 0)),
            pl.BlockSpec((None, D_MODEL, D_MODEL), lambda s: (wj, 0, 0), pipeline_mode=pl.Buffered(1)),
            proj_tile(D_MODEL),
            pl.BlockSpec((1, D_MODEL), lambda s: (0, 0)),
        ],
        out_specs=res_specs + [kv_spec, kv_spec],
        out_shape=res_shapes + [kv_shape, kv_shape],
        scratch_shapes=[pltpu.VMEM((WINDOW, A_KV_DIM), _f32), pltpu.VMEM((WINDOW, A_KV_DIM), _f32),
                        g_tile, g_tile, pltpu.VMEM((D_MODEL, D_MODEL), _bf16), pltpu.VMEM((rows, D_MODEL), _f32)],
        compiler_params=_cparams(1),
        name="attn_prompt",
    )(sinks, qk, vz, _prompt_bias_t(), w_out, x, gain)
    return outs[:-2], outs[-2], outs[-1]


def _attn_sample_kernel(n_aliased, sinks_ref, qk_ref, vz_ref, bias_ref, ck_ref, cv_ref, *refs):
    g_ref, ko_ref, vo_ref, kk_ref, vv_ref = refs[n_aliased:]
    q_all = _lane_chunks(qk_ref, 0, A_Q_DIM, dtype=_f32)
    k_all = qk_ref[:, A_Q_DIM:].astype(_f32)
    v_all = vz_ref[:, :A_KV_DIM].astype(_f32)
    z_all = _lane_chunks(vz_ref, A_KV_DIM, A_Q_DIM, dtype=_f32)
    keep = WINDOW - DEC_SEQ
    problems = []
    for s in range(SAMPLE_SEQS):
        rows = slice(s * DEC_SEQ, (s + 1) * DEC_SEQ)
        k_new, v_new = k_all[rows], v_all[rows]
        ko_ref[s, :keep] = ck_ref[s, DEC_SEQ:]
        ko_ref[s, keep:] = k_new
        vo_ref[s, :keep] = cv_ref[s, DEC_SEQ:]
        vo_ref[s, keep:] = v_new
        for ref, cache, new in ((kk_ref, ck_ref, k_new), (vv_ref, cv_ref, v_new)):
            ref[s, :WINDOW] = cache[s]
            ref[s, WINDOW:WINDOW + DEC_SEQ] = new
            ref[s, WINDOW + DEC_SEQ:] = jnp.zeros((WINDOW - DEC_SEQ, A_KV_DIM), _f32)
        problems.append(([q[rows] for q in q_all], kk_ref[s], vv_ref[s], [z[rows] for z in z_all]))
    g_parts = _attn_core_batch(problems, bias_ref[...], sinks_ref)
    for c in range(A_Q_DIM // LANES):
        g_ref[:, c * LANES:(c + 1) * LANES] = jnp.concatenate([p[c] for p in g_parts], axis=0).astype(g_ref.dtype)


def _sample_bias():
    t = jnp.arange(DEC_SEQ)[:, None]
    c = jnp.arange(2 * WINDOW)[None, :]
    ok = (c > t) & (c <= t + WINDOW)
    return jnp.where(ok, 0.0, NEG_BIG).astype(_f32)


def _attn_sample(qk, vz, sinks, cache_k, cache_v, layer, kv_out=None):
    srows = SAMPLE_SEQS * DEC_SEQ
    row0 = P_ROWS // srows
    aliased = () if kv_out is None else tuple(kv_out)
    n_in = 6
    kv_spec = pl.BlockSpec((None, SAMPLE_SEQS, WINDOW, A_KV_DIM), lambda b: (layer, b, 0, 0))
    kv_shape = jax.ShapeDtypeStruct((N_A_LAYERS, DEC_BATCH, WINDOW, A_KV_DIM), _f32)
    return pl.pallas_call(
        functools.partial(_attn_sample_kernel, len(aliased)),
        grid=(DEC_BATCH // SAMPLE_SEQS,),
        in_specs=[
            pl.BlockSpec(memory_space=pltpu.SMEM),
            pl.BlockSpec((srows, A_HALF_DIM), lambda b: (row0 + b, 0)),
            pl.BlockSpec((srows, A_HALF_DIM), lambda b: (row0 + b, 0)),
            pl.BlockSpec((DEC_SEQ, 2 * WINDOW), lambda b: (0, 0)),
            kv_spec, kv_spec,
        ] + [pl.BlockSpec(memory_space=pl.ANY)] * len(aliased),
        out_specs=[pl.BlockSpec((srows, A_Q_DIM), lambda b: (b, 0)), kv_spec, kv_spec],
        out_shape=[jax.ShapeDtypeStruct((S_ROWS, A_Q_DIM), _bf16), kv_shape, kv_shape],
        scratch_shapes=[pltpu.VMEM((SAMPLE_SEQS, 2 * WINDOW, A_KV_DIM), _f32)] * 2,
        input_output_aliases={n_in + k: k + 1 for k in range(len(aliased))},
        compiler_params=_cparams(1),
        name="attn_sample",
    )(sinks, qk, vz, _sample_bias(), cache_k, cache_v, *aliased)


def _out_proj_sample_kernel(final, n_aliased, a_ref, w_ref, x_ref, gain_ref, *refs):
    out_refs = refs[n_aliased:]
    xn = x_ref[...] + jnp.dot(a_ref[...], w_ref[...].astype(_bf16), preferred_element_type=_f32)
    if final:
        out_refs[0][...] = _rms_scale(xn, gain_ref[...])
    else:
        out_refs[0][...] = xn
        out_refs[1][...] = _rms_scale(xn, gain_ref[...]).astype(out_refs[1].dtype)


def _out_proj_sample(a, w, wj, x, x_block, gain, into=None):
    final = into is None
    aliased = () if final else tuple(into)
    sample_block = pl.BlockSpec((S_ROWS, D_MODEL), lambda m: (P_ROWS // S_ROWS, 0))
    whole = pl.BlockSpec((S_ROWS, D_MODEL), lambda m: (0, 0))
    return pl.pallas_call(
        functools.partial(_out_proj_sample_kernel, final, len(aliased)),
        grid=(1,),
        in_specs=[whole,
                  pl.BlockSpec((None, D_MODEL, D_MODEL), lambda m: (wj, 0, 0), pipeline_mode=pl.Buffered(1)),
                  pl.BlockSpec((S_ROWS, D_MODEL), lambda m: (x_block, 0)),
                  pl.BlockSpec((1, D_MODEL), lambda m: (0, 0))] + [pl.BlockSpec(memory_space=pl.ANY)] * len(aliased),
        out_specs=[whole] if final else [sample_block, sample_block],
        out_shape=[jax.ShapeDtypeStruct((S_ROWS, D_MODEL), _f32)] if final else _xn_h_shapes(),
        input_output_aliases={4 + k: k for k in range(len(aliased))},
        compiler_params=_cparams(1),
        name="out_proj_sample",
    )(a, w, x, gain, *aliased)


CONV_COLS = 256


def _conv_taps(u, u1, u2, w_ref):
    return w_ref[0:1, :] * u2 + w_ref[1:2, :] * u1 + w_ref[2:3, :] * u


CONV_ROW_PARTS = 3
CONV_PART = MM_ROW_TILE // CONV_ROW_PARTS
CONV_TAIL_ROWS = [b * SEQ + SEQ - SUBLANES for b in range(BATCH)]
SAMPLE_ROW0_IN_TILE = P_ROWS - (ROWS // MM_ROW_TILE - 1) * MM_ROW_TILE


def _conv_mixer_kernel(h_ref, w_bg_ref, w_cg_ref, w_x_ref, w_z_ref, cw_ref, p1_ref, p2_ref,
                       g_ref, tail_ref, us_ref, wcat_ref, carry_ref):
    m = pl.program_id(1)

    @pl.when(m == 0)
    def _():
        for k, w_ref in enumerate((w_bg_ref, w_cg_ref, w_x_ref, w_z_ref)):
            wcat_ref[:, k * CONV_COLS:(k + 1) * CONV_COLS] = w_ref[...].astype(_bf16)
        carry_ref[...] = jnp.zeros_like(carry_ref)

    prev2 = carry_ref[SUBLANES - 2:SUBLANES - 1, :]
    prev1 = carry_ref[SUBLANES - 1:SUBLANES, :]
    for r in range(CONV_ROW_PARTS):
        rows = slice(r * CONV_PART, (r + 1) * CONV_PART)
        y = jnp.dot(h_ref[rows, :], wcat_ref[...], preferred_element_type=_f32)
        bg, cg, xi, z = [y[:, k * CONV_COLS:(k + 1) * CONV_COLS] for k in range(4)]
        u = cg * xi
        local = lax.broadcasted_iota(jnp.int32, u.shape, 0)
        row = m * MM_ROW_TILE + r * CONV_PART + local
        is_sample = row >= P_ROWS
        step = jnp.where(is_sample, row % DEC_SEQ, row % SEQ)
        p1 = p2 = 0.0
        lo = max(SAMPLE_ROW0_IN_TILE, r * CONV_PART) - r * CONV_PART
        if lo < CONV_PART:
            head = jnp.zeros((lo, CONV_COLS), _f32)
            past = slice(lo + r * CONV_PART - SAMPLE_ROW0_IN_TILE, None)
            p1 = jnp.where(is_sample, jnp.concatenate([head, p1_ref[past, :]], axis=0), 0.0)
            p2 = jnp.where(is_sample, jnp.concatenate([head, p2_ref[past, :]], axis=0), 0.0)
        u1 = jnp.where(local == 0, prev1, pltpu.roll(u, 1, 0))
        u2 = jnp.where(local == 0, prev2, jnp.where(local == 1, prev1, pltpu.roll(u, 2, 0)))
        u1 = jnp.where(step == 0, p1, u1)
        u2 = jnp.where(step <= 1, p2, u2)
        conv = _conv_taps(u, u1, u2, cw_ref)
        g_ref[rows, :] = (_silu(z) * bg * conv).astype(g_ref.dtype)
        prev2, prev1 = u[CONV_PART - 2:CONV_PART - 1, :], u[CONV_PART - 1:CONV_PART, :]
        for b, tail_row in enumerate(CONV_TAIL_ROWS):
            off = tail_row % MM_ROW_TILE - r * CONV_PART
            if 0 <= off < CONV_PART:
                hit = jnp.broadcast_to(m == tail_row // MM_ROW_TILE, (SUBLANES, CONV_COLS))
                pltpu.store(tail_ref.at[b], u[off:off + SUBLANES], mask=hit)
        lo = max(SAMPLE_ROW0_IN_TILE, r * CONV_PART)
        hi = (r + 1) * CONV_PART
        if lo < hi:
            us_ref[lo - SAMPLE_ROW0_IN_TILE:hi - SAMPLE_ROW0_IN_TILE, :] = u[lo - r * CONV_PART:]
    carry_ref[...] = u[CONV_PART - SUBLANES:]


def _conv_mixer(h, w, wj, conv_w, p1, p2):
    ncb = D_MODEL // CONV_COLS

    def wseg(k):
        return pl.BlockSpec((None, D_MODEL, CONV_COLS), lambda c, m: (wj, 0, k * ncb + c))

    tile_cols = pl.BlockSpec((S_ROWS, CONV_COLS), lambda c, m: (0, c))
    return pl.pallas_call(
        _conv_mixer_kernel,
        grid=(ncb, ROWS // MM_ROW_TILE),
        in_specs=[pl.BlockSpec((MM_ROW_TILE, D_MODEL), lambda c, m: (m, 0)),
                  wseg(0), wseg(1), wseg(2), wseg(3),
                  pl.BlockSpec((CONV_WIDTH, CONV_COLS), lambda c, m: (0, c)), tile_cols, tile_cols],
        out_specs=[pl.BlockSpec((MM_ROW_TILE, CONV_COLS), lambda c, m: (m, c)),
                   pl.BlockSpec((BATCH, SUBLANES, CONV_COLS), lambda c, m: (0, 0, c)),
                   pl.BlockSpec((S_ROWS, CONV_COLS), lambda c, m: (0, c))],
        out_shape=[jax.ShapeDtypeStruct((ROWS, D_MODEL), _bf16),
                   jax.ShapeDtypeStruct((BATCH, SUBLANES, D_MODEL), _f32),
                   jax.ShapeDtypeStruct((S_ROWS, D_MODEL), _f32)],
        scratch_shapes=[pltpu.VMEM((D_MODEL, 4 * CONV_COLS), _bf16), pltpu.VMEM((SUBLANES, CONV_COLS), _f32)],
        compiler_params=_cparams(2),
        name="conv_mixer",
    )(h, w, w, w, w, conv_w, p1, p2)


def _conv_past_rows(state):
    pad_to_seq = lambda a: jnp.pad(a, ((0, 0), (0, DEC_SEQ - a.shape[1]), (0, 0))).reshape(S_ROWS, D_MODEL)
    return pad_to_seq(state[:, 1:2]), pad_to_seq(state)


HG_ROWS = 256
HG_INTER_LAG = 1


def _lower_bound(logit_ref, layer):
    lg = logit_ref[...]
    e = jnp.exp(lg - jnp.max(lg, axis=0, keepdims=True))
    return jnp.sum(e[1:layer + 1], axis=0, keepdims=True) / jnp.sum(e, axis=0, keepdims=True)


def _hgrn_gates(q, fl, lb):
    f = lb + (1.0 - lb) * jax.nn.sigmoid(fl)
    return _silu(q), 1.0 - f, jnp.log2(f)


def _chunk_cumsum(x, chunk):
    t = lax.broadcasted_iota(jnp.int32, x.shape, 0) % chunk
    s = 1
    while s < chunk:
        x = x + jnp.where(t >= s, pltpu.roll(x, s, 0), 0.0)
        s *= 2
    return x


def _hgrn_out(o, z, ng):
    ms = jnp.mean(o * o, axis=-1, keepdims=True)
    return _silu(z) * (o * lax.rsqrt(ms + RMS_EPS) * ng)


def _hgrn_prompt_prep(lb, q, fl, v):
    n_chunks = HG_ROWS // C_CHUNK
    qq, kk, lf = _hgrn_gates(q, fl, lb)
    b = _chunk_cumsum(lf, C_CHUNK)
    b_last = [b[(c + 1) * C_CHUNK - 1:(c + 1) * C_CHUNK] for c in range(n_chunks)]
    b_end = jnp.concatenate([jnp.broadcast_to(r, (C_CHUNK, C_KEY_DIM)) for r in b_last], axis=0)
    q_t = (qq * jnp.exp2(b)).astype(_bf16)
    k_intra = (kk * jnp.exp2(-b)).astype(_bf16)
    k_state = (kk * jnp.exp2(b_end - b)).astype(_bf16)
    return q_t, k_intra, k_state, v.astype(_bf16), b_last


def _hgrn_stage_scores(q_t, k_intra):
    nt = (((1,), (1,)), ((), ()))
    return lax.dot_general(q_t, k_intra, nt, preferred_element_type=_f32)


def _hgrn_stage_intra(scores, k_state, vb):
    tn = (((0,), (0,)), ((), ()))
    row = lax.broadcasted_iota(jnp.int32, (HG_ROWS, HG_ROWS), 0)
    col = lax.broadcasted_iota(jnp.int32, (HG_ROWS, HG_ROWS), 1)
    keep = (col <= row) & (col >= row - row % C_CHUNK)
    att = jnp.where(keep, scores, 0.0)
    o_intra = jnp.dot(att.astype(_bf16), vb, preferred_element_type=_f32)
    sl = [slice(c * C_CHUNK, (c + 1) * C_CHUNK) for c in range(HG_ROWS // C_CHUNK)]
    upd = [lax.dot_general(vb[r], k_state[r], tn, preferred_element_type=_f32) for r in sl]
    return o_intra, upd


def _hgrn_stage_inter(q_t, b_last, upd, st):
    nt = (((1,), (1,)), ((), ()))
    n_chunks = HG_ROWS // C_CHUNK
    sl = [slice(c * C_CHUNK, (c + 1) * C_CHUNK) for c in range(n_chunks)]
    starts = []
    for c in range(n_chunks):
        starts.append(st.astype(_bf16))
        st = st * jnp.exp2(b_last[c]) + upd[c]
    o_inter = jnp.concatenate([lax.dot_general(q_t[sl[c]], starts[c], nt, preferred_element_type=_f32)
                               for c in range(n_chunks)], axis=0)
    return o_inter, st


def _hgrn_prompt_kernel(layer, q_ref, f_ref, i_ref, z_ref, lg_ref, ng_ref, w_ref, x_ref, gain_ref,
                        xn_ref, h_ref, so_ref, st_ref, g_ref, gdone_ref, wb_ref, y_ref):
    s = pl.program_id(0)
    n_seq_tiles = SEQ // HG_ROWS
    tt = s % n_seq_tiles

    @pl.when(s == 0)
    def _():
        wb_ref[...] = w_ref[...].astype(_bf16)
        gdone_ref[...] = jnp.zeros_like(gdone_ref)

    @pl.when(tt == 0)
    def _():
        st_ref[...] = jnp.zeros_like(st_ref)

    def finish():
        xn = x_ref[...] + y_ref[...]
        xn_ref[...] = xn
        h_ref[...] = _rms_scale(xn, gain_ref[...]).astype(h_ref.dtype)

    @pl.when(s == pl.num_programs(0) - 1)
    def _():
        y_ref[...] = jnp.dot(gdone_ref[...], wb_ref[...], preferred_element_type=_f32)
        finish()

    @pl.when(s < pl.num_programs(0) - 1)
    def _():
        lb_all = _lower_bound(lg_ref, layer)
        lanes = lambda hh: slice(hh * LANES, (hh + 1) * LANES)
        heads_per_group = C_HEADS // (D_MODEL // MXU_COLS)
        prepped, scored, intra = {}, {}, {}
        for t in range(C_HEADS + HG_INTER_LAG):
            if t < C_HEADS:
                q, fl, v = [r[:, lanes(t)].astype(_f32) for r in (q_ref, f_ref, i_ref)]
                prepped[t] = _hgrn_prompt_prep(lb_all[:, lanes(t)], q, fl, v)
                scored[t] = _hgrn_stage_scores(prepped[t][0], prepped[t][1])
            h1 = t - 1
            if 0 <= h1 < C_HEADS:
                _, _, k_state, vb, _ = prepped[h1]
                intra[h1] = _hgrn_stage_intra(scored.pop(h1), k_state, vb)
            h2 = t - HG_INTER_LAG
            if 0 <= h2 < C_HEADS:
                q_t, _, _, _, b_last = prepped.pop(h2)
                o_intra, upd = intra.pop(h2)
                o_inter, st = _hgrn_stage_inter(q_t, b_last, upd, st_ref[h2])
                st_ref[h2] = st
                g_ref[:, lanes(h2)] = _hgrn_out(o_inter + o_intra, z_ref[:, lanes(h2)].astype(_f32),
                                                ng_ref[...]).astype(g_ref.dtype)
            if t < C_HEADS and t % heads_per_group == heads_per_group - 1:
                cols = slice(t // heads_per_group * MXU_COLS, (t // heads_per_group + 1) * MXU_COLS)
                y_ref[:, cols] = jnp.dot(gdone_ref[...], wb_ref[:, cols], preferred_element_type=_f32)
        gdone_ref[...] = g_ref[...]
        finish()

        @pl.when(tt == n_seq_tiles - 1)
        def _():
            for hh in range(C_HEADS):
                so_ref[0, hh] = st_ref[hh].T


def _hgrn_prompt(proj, lb_logits, norm_g, layer, w_out, wj, x, gain):
    nt = SEQ // HG_ROWS
    n_tiles = BATCH * nt
    mix_tile = lambda s: jnp.minimum(s, n_tiles - 1)
    proj_tile = lambda width: pl.BlockSpec((HG_ROWS, width), lambda s: (jnp.maximum(s - 1, 0), 0))
    seg = lambda k: pl.BlockSpec((HG_ROWS, D_MODEL), lambda s: (mix_tile(s), k))
    g_tile = pltpu.VMEM((HG_ROWS, D_MODEL), _bf16)
    xn, h, states = pl.pallas_call(
        functools.partial(_hgrn_prompt_kernel, layer),
        grid=(n_tiles + 1,),
        in_specs=[seg(0), seg(1), seg(2), seg(3),
                  pl.BlockSpec((DEPTH, C_F_DIM), lambda s: (0, 0)),
                  pl.BlockSpec((1, C_VAL_DIM), lambda s: (0, 0)),
                  pl.BlockSpec((None, D_MODEL, D_MODEL), lambda s: (wj, 0, 0), pipeline_mode=pl.Buffered(1)),
                  proj_tile(D_MODEL),
                  pl.BlockSpec((1, D_MODEL), lambda s: (0, 0))],
        out_specs=[proj_tile(D_MODEL), proj_tile(D_MODEL),
                   pl.BlockSpec((1, C_HEADS, C_KEY_DIM, C_VAL_DIM), lambda s: (mix_tile(s) // nt, 0, 0, 0))],
        out_shape=_xn_h_shapes() + [jax.ShapeDtypeStruct((BATCH, C_HEADS, C_KEY_DIM, C_VAL_DIM), _f32)],
        scratch_shapes=[pltpu.VMEM((C_HEADS, C_VAL_DIM, C_KEY_DIM), _f32), g_tile, g_tile,
                        pltpu.VMEM((D_MODEL, D_MODEL), _bf16), pltpu.VMEM((HG_ROWS, D_MODEL), _f32)],
        compiler_params=_cparams(1),
        name="hgrn_prompt",
    )(proj, proj, proj, proj, lb_logits, norm_g, w_out, x, gain)
    return (xn, h), states


def _hgrn_sample_kernel(layer, proj_ref, lg_ref, ng_ref, s_ref, g_ref, so_ref):
    nt = (((1,), (1,)), ((), ()))
    tn = (((0,), (0,)), ((), ()))
    row = lax.broadcasted_iota(jnp.int32, (DEC_SEQ, DEC_SEQ), 0)
    col = lax.broadcasted_iota(jnp.int32, (DEC_SEQ, DEC_SEQ), 1)
    causal = row >= col
    eye = (lax.broadcasted_iota(jnp.int32, (C_KEY_DIM, C_KEY_DIM), 0)
           == lax.broadcasted_iota(jnp.int32, (C_KEY_DIM, C_KEY_DIM), 1))
    for h in range(C_HEADS):
        hs = slice(h * LANES, (h + 1) * LANES)
        lb = _lower_bound(lg_ref.at[:, hs], layer)
        q, fl, v, z = [proj_ref[:, off + h * LANES:off + (h + 1) * LANES].astype(_f32)
                       for off in (0, C_F_DIM, 2 * C_F_DIM, 2 * C_F_DIM + D_MODEL)]
        qq, kk, lf = _hgrn_gates(q, fl, lb)
        b = _chunk_cumsum(lf, DEC_SEQ)
        q_t = qq * jnp.exp2(b)
        k_intra = kk * jnp.exp2(-b)
        outs = []
        for s in range(SAMPLE_SEQS):
            rows = slice(s * DEC_SEQ, (s + 1) * DEC_SEQ)
            b_last = b[(s + 1) * DEC_SEQ - 1:(s + 1) * DEC_SEQ]
            qb = q_t[rows].astype(_bf16)
            k_state = (kk[rows] * jnp.exp2(b_last - b[rows])).astype(_bf16)
            vb = v[rows].astype(_bf16)
            s0 = s_ref[s, h]
            o_inter = jnp.dot(qb, s0.astype(_bf16), preferred_element_type=_f32)
            att = jnp.where(causal, lax.dot_general(qb, k_intra[rows].astype(_bf16), nt,
                                                    preferred_element_type=_f32), 0.0)
            outs.append(o_inter + jnp.dot(att.astype(_bf16), vb, preferred_element_type=_f32))
            decay = jnp.sum(jnp.where(eye, jnp.broadcast_to(jnp.exp2(b_last), eye.shape), 0.0),
                            axis=1, keepdims=True)
            so_ref[s, h] = decay * s0 + lax.dot_general(k_state, vb, tn, preferred_element_type=_f32)
        g_ref[:, hs] = _hgrn_out(jnp.concatenate(outs, axis=0), z, ng_ref[...]).astype(g_ref.dtype)


def _hgrn_sample(proj, lb_logits, norm_g, state, layer):
    srows = SAMPLE_SEQS * DEC_SEQ
    row0 = P_ROWS // srows
    return pl.pallas_call(
        functools.partial(_hgrn_sample_kernel, layer),
        grid=(DEC_BATCH // SAMPLE_SEQS,),
        in_specs=[pl.BlockSpec((srows, C_IN_DIM), lambda b: (row0 + b, 0)),
                  pl.BlockSpec((DEPTH, C_F_DIM), lambda b: (0, 0)),
                  pl.BlockSpec((1, C_VAL_DIM), lambda b: (0, 0)),
                  pl.BlockSpec((SAMPLE_SEQS, C_HEADS, C_KEY_DIM, C_VAL_DIM), lambda b: (b, 0, 0, 0))],
        out_specs=[pl.BlockSpec((srows, D_MODEL), lambda b: (b, 0)),
                   pl.BlockSpec((SAMPLE_SEQS, C_HEADS, C_KEY_DIM, C_VAL_DIM), lambda b: (b, 0, 0, 0))],
        out_shape=[jax.ShapeDtypeStruct((S_ROWS, D_MODEL), _bf16),
                   jax.ShapeDtypeStruct((DEC_BATCH, C_HEADS, C_KEY_DIM, C_VAL_DIM), _f32)],
        compiler_params=_cparams(1),
        name="hgrn_sample",
    )(proj, lb_logits, norm_g, state)


def kernel(x_prompt, x_sample, cache_k, cache_v, state_conv, state_hgrn, ln_g, final_g, a_w_in, a_w_out, a_sinks,
           b_w_in, b_conv_w, b_w_out, c_w_in, c_norm_g, c_w_out, c_lb_logits):
    xp = x_prompt.reshape(P_ROWS, D_MODEL)
    xs = x_sample.reshape(S_ROWS, D_MODEL)
    tabs = jnp.asarray(_rope_tables())
    norm_gains = [ln_g[li].reshape(1, D_MODEL) for li in range(1, DEPTH)] + [final_g.reshape(1, D_MODEL)]

    h = _first_norm(xp, xs, ln_g[0].reshape(1, D_MODEL))
    x = None
    ks_p, vs_p, conv_p, conv_s, hg_p, hg_s = [], [], [], [], [], []
    kv_sample = None
    caches = [c.reshape(N_A_LAYERS, DEC_BATCH, WINDOW, A_KV_DIM) for c in (cache_k, cache_v)]
    for li in range(DEPTH):
        kind, j = li % 3, li // 3
        if kind == 0:
            qk = _in_proj(h, a_w_in, j, A_HALF_DIM, A_COL_TILE, rope_tabs=tabs)
            vz = _in_proj(h, a_w_in, j, A_HALF_DIM, A_COL_TILE, col0=A_HALF_DIM)
            final = li == DEPTH - 1
            res, k_p, v_p = _attn_prompt(qk, vz, a_sinks[j], a_w_out, j, xp if li == 0 else x, norm_gains[li], final)
            g_s, *kv_sample = _attn_sample(qk, vz, a_sinks[j], caches[0], caches[1], j, kv_sample)
            ks_p.append(k_p), vs_p.append(v_p)
            x_s, x_block = (xs, 0) if li == 0 else (x, P_ROWS // S_ROWS)
            if final:
                (y_p,), (y_s,) = res, _out_proj_sample(g_s, a_w_out, j, x_s, x_block, norm_gains[li])
            else:
                x, h = _out_proj_sample(g_s, a_w_out, j, x_s, x_block, norm_gains[li], into=res)
            continue
        elif kind == 1:
            g, tail, u_s = _conv_mixer(h, b_w_in, j, b_conv_w[j], *_conv_past_rows(state_conv[j]))
            conv_p.append(tail[:, SUBLANES - (CONV_WIDTH - 1):])
            conv_s.append(u_s.reshape(DEC_BATCH, DEC_SEQ, D_MODEL)[:, DEC_SEQ - (CONV_WIDTH - 1):])
            w_out = b_w_out
        else:
            proj = _in_proj(h, c_w_in, j, C_IN_DIM, MM_COL_TILE)
            ng = c_norm_g[j].reshape(1, C_VAL_DIM)
            res, s_p = _hgrn_prompt(proj, c_lb_logits, ng, li, c_w_out, j, x, norm_gains[li])
            g_s, s_s = _hgrn_sample(proj, c_lb_logits, ng, state_hgrn[j], li)
            hg_p.append(s_p), hg_s.append(s_s)
            x, h = _out_proj_sample(g_s, c_w_out, j, x, P_ROWS // S_ROWS, norm_gains[li], into=res)
            continue
        x, h = _out_proj(g, w_out, j, x, norm_gains[li])

    def stacked(parts, *shape):
        arr = parts[0][None] if len(parts) == 1 else jnp.stack(parts)
        return arr.reshape(len(parts), *shape)

    kv_shape = (WINDOW, A_KV_HEADS, A_HEAD_DIM)
    return (y_p.reshape(BATCH, SEQ, D_MODEL), y_s.reshape(DEC_BATCH, DEC_SEQ, D_MODEL),
            stacked(ks_p, BATCH, *kv_shape), stacked(vs_p, BATCH, *kv_shape),
            kv_sample[0].reshape(N_A_LAYERS, DEC_BATCH, *kv_shape),
            kv_sample[1].reshape(N_A_LAYERS, DEC_BATCH, *kv_shape),
            stacked(conv_p, BATCH, CONV_WIDTH - 1, D_MODEL), stacked(conv_s, DEC_BATCH, CONV_WIDTH - 1, D_MODEL),
            stacked(hg_p, BATCH, C_HEADS, C_KEY_DIM, C_VAL_DIM),
            stacked(hg_s, DEC_BATCH, C_HEADS, C_KEY_DIM, C_VAL_DIM))
```

```python
import functools

import numpy as np

import jax
import jax.numpy as jnp
from jax import lax
from jax.experimental import pallas as pl
from jax.experimental.pallas import tpu as pltpu

D_MODEL = 2048
BATCH = 4
SEQ = 2048
DEPTH = 4
N_A_LAYERS = (DEPTH + 2) // 3
DEC_BATCH = 32
DEC_SEQ = 8
PAST_LEN = 16384
RMS_EPS = 1e-6

A_HEADS = 32
A_KV_HEADS = 4
A_HEAD_DIM = 64
A_GROUP = A_HEADS // A_KV_HEADS
A_Q_DIM = A_HEADS * A_HEAD_DIM
A_KV_DIM = A_KV_HEADS * A_HEAD_DIM
WINDOW = 128
ROT_DIM = A_HEAD_DIM // 4
ROPE_THETA = 500000.0

CONV_WIDTH = 3

C_HEADS = 16
C_KEY_DIM = 128
C_VAL_DIM = D_MODEL // C_HEADS
C_F_DIM = C_HEADS * C_KEY_DIM
C_IN_DIM = 2 * C_F_DIM + 2 * D_MODEL
C_CHUNK = 32

LANES = 128
SUBLANES = 8
MXU_COLS = 256
P_ROWS = BATCH * SEQ
S_ROWS = DEC_BATCH * DEC_SEQ
ROWS = P_ROWS + S_ROWS
ROW_TILE = 384
SAMPLE_SEQS = 4
MM_ROW_TILE = ROWS // 4
MM_COL_TILE = 1024
ROPE_ROW_PARTS = 6
NEG_BIG = -1e30
LOG2E = 1.4426950408889634
Q_SCALE = A_HEAD_DIM ** -0.5 * LOG2E
VMEM_LIMIT = 56 * 1024 * 1024

_f32 = jnp.float32
_bf16 = jnp.bfloat16


def _cparams(n_axes):
    return pltpu.CompilerParams(dimension_semantics=("arbitrary",) * n_axes, vmem_limit_bytes=VMEM_LIMIT)


def _silu(x):
    return x * jax.nn.sigmoid(x)


def _rms_scale(x, g):
    ms = jnp.mean(x * x, axis=-1, keepdims=True)
    return x * lax.rsqrt(ms + RMS_EPS) * g


NORM_ROW_TILE = 1024


def _first_norm_kernel(x_ref, g_ref, h_ref):
    h_ref[...] = _rms_scale(x_ref[...], g_ref[...]).astype(h_ref.dtype)


def _first_norm_sample_kernel(x_ref, g_ref, hin_ref, h_ref):
    del hin_ref
    h_ref[...] = _rms_scale(x_ref[...], g_ref[...]).astype(h_ref.dtype)


def _first_norm(xp, xs, g):
    gain_spec = pl.BlockSpec((1, D_MODEL), lambda m: (0, 0))
    h = pl.pallas_call(
        _first_norm_kernel,
        grid=(P_ROWS // NORM_ROW_TILE,),
        in_specs=[pl.BlockSpec((NORM_ROW_TILE, D_MODEL), lambda m: (m, 0)), gain_spec],
        out_specs=pl.BlockSpec((NORM_ROW_TILE, D_MODEL), lambda m: (m, 0)),
        out_shape=jax.ShapeDtypeStruct((ROWS, D_MODEL), _bf16),
        compiler_params=_cparams(1),
        name="first_norm",
    )(xp, g)
    return pl.pallas_call(
        _first_norm_sample_kernel,
        grid=(1,),
        in_specs=[pl.BlockSpec((S_ROWS, D_MODEL), lambda m: (0, 0)), gain_spec, pl.BlockSpec(memory_space=pl.ANY)],
        out_specs=pl.BlockSpec((S_ROWS, D_MODEL), lambda m: (P_ROWS // S_ROWS, 0)),
        out_shape=jax.ShapeDtypeStruct((ROWS, D_MODEL), _bf16),
        input_output_aliases={2: 0},
        compiler_params=_cparams(1),
        name="first_norm_sample",
    )(xs, g, h)


def _cast_weight_once(w_ref, wb_ref):
    @pl.when(pl.program_id(1) == 0)
    def _():
        wb_ref[...] = w_ref[...].astype(_bf16)


def _in_proj_kernel(h_ref, w_ref, o_ref, wb_ref):
    _cast_weight_once(w_ref, wb_ref)
    part = MM_ROW_TILE // ROPE_ROW_PARTS
    for r in range(ROPE_ROW_PARTS):
        rows = slice(r * part, (r + 1) * part)
        o_ref[rows, :] = jnp.dot(h_ref[rows, :], wb_ref[...], preferred_element_type=_f32).astype(o_ref.dtype)


def _in_proj_rope_kernel(tn, h_ref, w_ref, tab_ref, o_ref, wb_ref):
    _cast_weight_once(w_ref, wb_ref)
    half = ROT_DIM // 2
    chunks = tn // LANES
    part = MM_ROW_TILE // ROPE_ROW_PARTS
    for r in range(ROPE_ROW_PARTS):
        rows = slice(r * part, (r + 1) * part)
        y = jnp.dot(h_ref[rows, :], wb_ref[...], preferred_element_type=_f32)
        tabs = [tab_ref[k, rows, :] for k in range(3)]
        for c in range(chunks):
            yc = y[:, c * LANES:(c + 1) * LANES]
            is_query = pl.program_id(0) * chunks + c < A_Q_DIM // LANES
            scale = jnp.where(is_query, Q_SCALE, 1.0)
            rc = yc * tabs[0] + pltpu.roll(yc, LANES - half, 1) * tabs[1] + pltpu.roll(yc, half, 1) * tabs[2]
            o_ref[rows, c * LANES:(c + 1) * LANES] = (rc * scale).astype(o_ref.dtype)


def _in_proj(h, w, wj, n_out, tn, col0=0, rope_tabs=None):
    cb0 = col0 // tn
    in_specs = [pl.BlockSpec((MM_ROW_TILE, D_MODEL), lambda j, m: (m, 0)),
                pl.BlockSpec((None, D_MODEL, tn), lambda j, m: (wj, 0, cb0 + j))]
    args = [h, w]
    if rope_tabs is None:
        body = _in_proj_kernel
    else:
        body = functools.partial(_in_proj_rope_kernel, tn)
        in_specs.append(pl.BlockSpec((3, MM_ROW_TILE, LANES), lambda j, m: (0, m, 0)))
        args.append(rope_tabs)
    return pl.pallas_call(
        body,
        grid=(n_out // tn, ROWS // MM_ROW_TILE),
        in_specs=in_specs,
        out_specs=pl.BlockSpec((MM_ROW_TILE, tn), lambda j, m: (m, j)),
        out_shape=jax.ShapeDtypeStruct((ROWS, n_out), _bf16),
        scratch_shapes=[pltpu.VMEM((D_MODEL, tn), _bf16)],
        compiler_params=_cparams(2),
        name="in_proj" if rope_tabs is None else "in_proj_rope",
    )(*args)


def _out_proj_kernel(a_ref, w_ref, x_ref, g_ref, xn_ref, h_ref, wb_ref):
    @pl.when(pl.program_id(0) == 0)
    def _():
        wb_ref[...] = w_ref[...].astype(_bf16)

    xn = x_ref[...] + jnp.dot(a_ref[...], wb_ref[...], preferred_element_type=_f32)
    xn_ref[...] = xn
    h_ref[...] = _rms_scale(xn, g_ref[...]).astype(h_ref.dtype)


def _xn_h_shapes():
    return [jax.ShapeDtypeStruct((ROWS, D_MODEL), _f32), jax.ShapeDtypeStruct((ROWS, D_MODEL), _bf16)]


def _out_proj(a, w, wj, x, g):
    tile = pl.BlockSpec((ROW_TILE, D_MODEL), lambda m: (m, 0))
    return pl.pallas_call(
        _out_proj_kernel,
        grid=(ROWS // ROW_TILE,),
        in_specs=[tile,
                  pl.BlockSpec((None, D_MODEL, D_MODEL), lambda m: (wj, 0, 0), pipeline_mode=pl.Buffered(1)),
                  tile, pl.BlockSpec((1, D_MODEL), lambda m: (0, 0))],
        out_specs=[tile, tile],
        out_shape=_xn_h_shapes(),
        scratch_shapes=[pltpu.VMEM((D_MODEL, D_MODEL), _bf16)],
        compiler_params=_cparams(1),
        name="out_proj",
    )(a, w, x, g)


def _rope_tables():
    pos = np.concatenate([np.arange(P_ROWS) % SEQ, PAST_LEN + np.arange(S_ROWS) % DEC_SEQ]).astype(np.float64)
    half = ROT_DIM // 2
    inv = ROPE_THETA ** (-np.arange(half, dtype=np.float64) * 2.0 / ROT_DIM)
    ang = pos[:, None] * inv[None, :]
    cos, sin = np.cos(ang), np.sin(ang)
    rest = A_HEAD_DIM - ROT_DIM
    c_head = np.concatenate([cos, cos, np.ones((ROWS, rest))], axis=1)
    s1_head = np.concatenate([-sin, np.zeros((ROWS, half + rest))], axis=1)
    s2_head = np.concatenate([np.zeros((ROWS, half)), sin, np.zeros((ROWS, rest))], axis=1)
    rep = LANES // A_HEAD_DIM
    return np.stack([np.tile(t, (1, rep)) for t in (c_head, s1_head, s2_head)]).astype(np.float32)


def _lane_halves(xc, head_in_low_half):
    lane = lax.broadcasted_iota(jnp.int32, xc.shape, 1)
    low = lane < A_HEAD_DIM
    sw = pltpu.roll(xc, A_HEAD_DIM, 1)
    if head_in_low_half:
        return jnp.where(low, xc, 0.0), jnp.where(low, 0.0, sw)
    return jnp.where(low, sw, 0.0), jnp.where(low, 0.0, xc)


def _attn_core_batch(problems, bias, sinks_ref):
    nt = (((1,), (1,)), ((), ()))
    cols_per_kv = A_GROUP * A_HEAD_DIM // LANES
    tq = problems[0][0][0].shape[0]
    lane = lax.broadcasted_iota(jnp.int32, (tq, LANES), 1)
    low = lane < A_HEAD_DIM
    pairs = [(n, j) for n in range(len(problems)) for j in range(A_KV_HEADS)]
    scores, values = {}, {}
    for n, j in pairs:
        q_cols, kk, vv, _ = problems[n]
        kc = kk[:, (j // 2) * LANES:(j // 2 + 1) * LANES]
        vc = vv[:, (j // 2) * LANES:(j // 2 + 1) * LANES]
        k_lo, k_hi = _lane_halves(kc, j % 2 == 0)
        values[n, j] = _lane_halves(vc, j % 2 == 0)
        qs = jnp.concatenate(q_cols[j * cols_per_kv:(j + 1) * cols_per_kv], axis=0).astype(_bf16)
        scores[n, j] = (lax.dot_general(qs, k_lo.astype(_bf16), nt, preferred_element_type=_f32),
                        lax.dot_general(qs, k_hi.astype(_bf16), nt, preferred_element_type=_f32))
    probs = {}
    for n, j in pairs:
        p_lo, p_hi, r_lo, r_hi = [], [], [], []
        for c in range(cols_per_kv):
            for s_all, ps, rs, odd in ((scores[n, j][0], p_lo, r_lo, 0), (scores[n, j][1], p_hi, r_hi, 1)):
                sink = sinks_ref[j * A_GROUP + 2 * c + odd] * LOG2E
                s = s_all[c * tq:(c + 1) * tq] + bias
                m = jnp.maximum(jnp.max(s, axis=-1, keepdims=True), sink)
                p = jnp.exp2(s - m)
                den = jnp.sum(p, axis=-1, keepdims=True) + jnp.exp2(sink - m)
                ps.append(p)
                rs.append(1.0 / den)
        probs[n, j] = (jnp.concatenate(p_lo, axis=0).astype(_bf16), jnp.concatenate(p_hi, axis=0).astype(_bf16),
                       r_lo, r_hi)
    outs = [[] for _ in problems]
    for n, j in pairs:
        p_lo, p_hi, r_lo, r_hi = probs[n, j]
        v_lo, v_hi = values[n, j]
        o = (jnp.dot(p_lo, v_lo.astype(_bf16), preferred_element_type=_f32)
             + jnp.dot(p_hi, v_hi.astype(_bf16), preferred_element_type=_f32))
        z_cols = problems[n][3]
        for c in range(cols_per_kv):
            oc = o[c * tq:(c + 1) * tq] * jnp.where(low, r_lo[c], r_hi[c])
            outs[n].append(_silu(z_cols[j * cols_per_kv + c]) * oc)
    return outs


Q_COLS_PER_KV = A_GROUP * A_HEAD_DIM // LANES


def _attn_stage_scores(q_cols, kk, j):
    nt = (((1,), (1,)), ((), ()))
    k_lo, k_hi = _lane_halves(kk[:, (j // 2) * LANES:(j // 2 + 1) * LANES], j % 2 == 0)
    qs = jnp.concatenate(q_cols, axis=0).astype(_bf16)
    return (lax.dot_general(k_lo.astype(_bf16), qs, nt, preferred_element_type=_f32),
            lax.dot_general(k_hi.astype(_bf16), qs, nt, preferred_element_type=_f32))


def _attn_stage_values(scores, vv, j, bias_t, sinks_ref):
    tq = WINDOW
    zeros = jnp.zeros((A_HEAD_DIM, 2 * WINDOW), _f32)
    srow = lax.broadcasted_iota(jnp.int32, (2 * SUBLANES, 4 * WINDOW), 0)
    scol = lax.broadcasted_iota(jnp.int32, (2 * SUBLANES, 4 * WINDOW), 1)
    sum_rows = jnp.where(srow == scol // (2 * WINDOW), 1.0, 0.0)
    p_lo, p_hi, r_lo, r_hi = [], [], [], []
    for c in range(Q_COLS_PER_KV):
        for st_all, ps, rs, odd in ((scores[0], p_lo, r_lo, 0), (scores[1], p_hi, r_hi, 1)):
            sink = sinks_ref[j * A_GROUP + 2 * c + odd] * LOG2E
            s = st_all[:, c * tq:(c + 1) * tq] + bias_t
            m = jnp.maximum(jnp.max(s, axis=0, keepdims=True), sink)
            ps.append(jnp.exp2(s - m).astype(_bf16))
            rs.append(jnp.exp2(sink - m))
    pt = jnp.concatenate([jnp.concatenate(p_lo, axis=1), jnp.concatenate(p_hi, axis=1)], axis=0)
    vt = vv[:, (j // 2) * LANES:(j // 2 + 1) * LANES].T
    vj = vt[:A_HEAD_DIM] if j % 2 == 0 else vt[A_HEAD_DIM:]
    vcat = jnp.concatenate([jnp.concatenate([vj, zeros], axis=1), jnp.concatenate([zeros, vj], axis=1),
                            sum_rows], axis=0)
    return jnp.dot(vcat.astype(_bf16), pt, preferred_element_type=_f32), (r_lo, r_hi)


def _attn_stage_output(ot, sink_terms, z_cols):
    tq = WINDOW
    low = lax.broadcasted_iota(jnp.int32, (LANES, tq), 0) < A_HEAD_DIM
    out = []
    for c in range(Q_COLS_PER_KV):
        cols = slice(c * tq, (c + 1) * tq)
        den_lo = ot[LANES:LANES + 1, cols] + sink_terms[0][c]
        den_hi = ot[LANES + 1:LANES + 2, cols] + sink_terms[1][c]
        oc = ot[:LANES, cols] * jnp.where(low, 1.0 / den_lo, 1.0 / den_hi)
        out.append(_silu(z_cols[c]) * oc.T)
    return out


def _lane_chunks(ref, start, width, rows=slice(None), dtype=None):
    chunks = [ref[rows, start + c * LANES:start + (c + 1) * LANES] for c in range(width // LANES)]
    return chunks if dtype is None else [c.astype(dtype) for c in chunks]


def _attn_prompt_kernel(final, sinks_ref, qk_ref, vz_ref, bias_ref, w_ref, x_ref, gain_ref, *refs):
    out_refs, (ko_ref, vo_ref, kprev, vprev, g_ref, gdone_ref, wb_ref, y_ref) = refs[:-8], refs[-8:]
    s = pl.program_id(0)
    i = s % (SEQ // (ATTN_BLOCKS_PER_STEP * WINDOW))

    @pl.when(s == 0)
    def _():
        wb_ref[...] = w_ref[...].astype(_bf16)
        gdone_ref[...] = jnp.zeros_like(gdone_ref)

    @pl.when(i == 0)
    def _():
        kprev[...] = jnp.zeros_like(kprev)
        vprev[...] = jnp.zeros_like(vprev)

    def finish():
        xn = x_ref[...] + y_ref[...]
        if final:
            out_refs[0][...] = _rms_scale(xn, gain_ref[...])
        else:
            out_refs[0][...] = xn
            out_refs[1][...] = _rms_scale(xn, gain_ref[...]).astype(out_refs[1].dtype)

    @pl.when(s == pl.num_programs(0) - 1)
    def _():
        y_ref[...] = jnp.dot(gdone_ref[...], wb_ref[...], preferred_element_type=_f32)
        finish()

    @pl.when(s < pl.num_programs(0) - 1)
    def _():
        blocks = []
        k_prev, v_prev = kprev[...], vprev[...]
        for sub in range(ATTN_BLOCKS_PER_STEP):
            rows = slice(sub * WINDOW, (sub + 1) * WINDOW)
            k_new = qk_ref[rows, A_Q_DIM:].astype(_f32)
            v_new = vz_ref[rows, :A_KV_DIM].astype(_f32)
            bias = jnp.where(i == 0, bias_ref[0], bias_ref[1]) if sub == 0 else bias_ref[1]
            blocks.append((rows, jnp.concatenate([k_prev, k_new], axis=0), jnp.concatenate([v_prev, v_new], axis=0), bias))
            k_prev, v_prev = k_new, v_new
        ko_ref[0] = k_prev
        vo_ref[0] = v_prev
        kprev[...] = k_prev
        vprev[...] = v_prev

        items = [(sub, j) for sub in range(ATTN_BLOCKS_PER_STEP) for j in range(A_KV_HEADS)]
        assert len(items) == D_MODEL // MXU_COLS
        scores, values = {}, {}
        for t in range(len(items) + 2):
            if t < len(items):
                sub, j = items[t]
                rows, kk, _, _ = blocks[sub]
                q_cols = _lane_chunks(qk_ref, j * Q_COLS_PER_KV * LANES, Q_COLS_PER_KV * LANES, rows)
                scores[t] = _attn_stage_scores(q_cols, kk, j)
                cols = slice(t * MXU_COLS, (t + 1) * MXU_COLS)
                y_ref[:, cols] = jnp.dot(gdone_ref[...], wb_ref[:, cols], preferred_element_type=_f32)
            if 0 <= t - 1 < len(items):
                sub, j = items[t - 1]
                _, _, vv, bias = blocks[sub]
                values[t - 1] = _attn_stage_values(scores.pop(t - 1), vv, j, bias, sinks_ref)
            if 0 <= t - 2 < len(items):
                sub, j = items[t - 2]
                rows = blocks[sub][0]
                first = j * Q_COLS_PER_KV * LANES
                z_cols = _lane_chunks(vz_ref, A_KV_DIM + first, Q_COLS_PER_KV * LANES, rows, dtype=_f32)
                for c, gc in enumerate(_attn_stage_output(*values.pop(t - 2), z_cols)):
                    g_ref[rows, first + c * LANES:first + (c + 1) * LANES] = gc.astype(g_ref.dtype)
        gdone_ref[...] = g_ref[...]
        finish()


def _prompt_bias_t():
    c = jnp.arange(2 * WINDOW)[:, None]
    r = jnp.arange(WINDOW)[None, :]
    band = (c > r) & (c <= r + WINDOW)
    first = band & (c >= WINDOW)
    return jnp.where(jnp.stack([first, band]), 0.0, NEG_BIG).astype(_f32)


A_HALF_DIM = A_Q_DIM + A_KV_DIM
A_COL_TILE = A_HALF_DIM // 3


ATTN_BLOCKS_PER_STEP = 2


def _attn_prompt(qk, vz, sinks, w_out, wj, x, gain, final):
    rows = ATTN_BLOCKS_PER_STEP * WINDOW
    nb = SEQ // rows
    n_tiles = BATCH * nb
    attn_tile = lambda width: pl.BlockSpec((rows, width), lambda s: (jnp.minimum(s, n_tiles - 1), 0))
    proj_tile = lambda width: pl.BlockSpec((rows, width), lambda s: (jnp.maximum(s - 1, 0), 0))
    kv_spec = pl.BlockSpec((1, WINDOW, A_KV_DIM), lambda s: (jnp.minimum(s, n_tiles - 1) // nb, 0, 0))
    kv_shape = jax.ShapeDtypeStruct((BATCH, WINDOW, A_KV_DIM), _f32)
    if final:
        res_specs, res_shapes = [proj_tile(D_MODEL)], [jax.ShapeDtypeStruct((P_ROWS, D_MODEL), _f32)]
    else:
        res_specs, res_shapes = [proj_tile(D_MODEL), proj_tile(D_MODEL)], _xn_h_shapes()
    g_tile = pltpu.VMEM((rows, A_Q_DIM), _bf16)
    outs = pl.pallas_call(
        functools.partial(_attn_prompt_kernel, final),
        grid=(n_tiles + 1,),
        in_specs=[
            pl.BlockSpec(memory_space=pltpu.SMEM),
            attn_tile(A_HALF_DIM), attn_tile(A_HALF_DIM),
            pl.BlockSpec((2, 2 * WINDOW, WINDOW), lambda s: (0, 0, 0)),
            pl.BlockSpec((None, D_MODEL, D_MODEL), lambda s: (wj, 0, 0), pipeline_mode=pl.Buffered(1)),
            proj_tile(D_MODEL),
            pl.BlockSpec((1, D_MODEL), lambda s: (0, 0)),
        ],
        out_specs=res_specs + [kv_spec, kv_spec],
        out_shape=res_shapes + [kv_shape, kv_shape],
        scratch_shapes=[pltpu.VMEM((WINDOW, A_KV_DIM), _f32), pltpu.VMEM((WINDOW, A_KV_DIM), _f32),
                        g_tile, g_tile, pltpu.VMEM((D_MODEL, D_MODEL), _bf16), pltpu.VMEM((rows, D_MODEL), _f32)],
        compiler_params=_cparams(1),
        name="attn_prompt",
    )(sinks, qk, vz, _prompt_bias_t(), w_out, x, gain)
    return outs[:-2], outs[-2], outs[-1]


def _project_sample_rows(final, g_ref, w_ref, x_ref, gain_ref, out_refs):
    g = g_ref[...]
    y = [jnp.dot(g, w_ref[:, c * MXU_COLS:(c + 1) * MXU_COLS].astype(_bf16), preferred_element_type=_f32)
         for c in range(D_MODEL // MXU_COLS)]
    xn = x_ref[...] + jnp.concatenate(y, axis=1)
    if final:
        out_refs[0][...] = _rms_scale(xn, gain_ref[...])
    else:
        out_refs[0][...] = xn
        out_refs[1][...] = _rms_scale(xn, gain_ref[...]).astype(out_refs[1].dtype)


def _sample_projection_specs(wj, x_block, into):
    once = lambda shape, index: pl.BlockSpec(shape, index, pipeline_mode=pl.Buffered(1))
    in_specs = [once((None, D_MODEL, D_MODEL), lambda b: (wj, 0, 0)),
                once((S_ROWS, D_MODEL), lambda b: (x_block, 0)),
                pl.BlockSpec((1, D_MODEL), lambda b: (0, 0))]
    if into is None:
        return in_specs, [pl.BlockSpec((S_ROWS, D_MODEL), lambda b: (0, 0))], [jax.ShapeDtypeStruct((S_ROWS, D_MODEL), _f32)]
    sample_block = pl.BlockSpec((S_ROWS, D_MODEL), lambda b: (P_ROWS // S_ROWS, 0))
    return in_specs, [sample_block, sample_block], _xn_h_shapes()


def _attn_sample_kernel(final, n_aliased, sinks_ref, qk_ref, vz_ref, bias_ref, ck_ref, cv_ref, w_ref, x_ref, gain_ref,
                        *refs):
    *res_refs, ko_ref, vo_ref, kk_ref, vv_ref, g_ref = refs[n_aliased:]
    step = pl.program_id(0)
    srows = SAMPLE_SEQS * DEC_SEQ
    step_rows = pl.ds(pl.multiple_of(step * srows, srows), srows)
    q_all = _lane_chunks(qk_ref, 0, A_Q_DIM, dtype=_f32)
    k_all = qk_ref[:, A_Q_DIM:].astype(_f32)
    v_all = vz_ref[:, :A_KV_DIM].astype(_f32)
    z_all = _lane_chunks(vz_ref, A_KV_DIM, A_Q_DIM, dtype=_f32)
    keep = WINDOW - DEC_SEQ
    problems = []
    for s in range(SAMPLE_SEQS):
        rows = slice(s * DEC_SEQ, (s + 1) * DEC_SEQ)
        k_new, v_new = k_all[rows], v_all[rows]
        ko_ref[s, :keep] = ck_ref[s, DEC_SEQ:]
        ko_ref[s, keep:] = k_new
        vo_ref[s, :keep] = cv_ref[s, DEC_SEQ:]
        vo_ref[s, keep:] = v_new
        for ref, cache, new in ((kk_ref, ck_ref, k_new), (vv_ref, cv_ref, v_new)):
            ref[s, :WINDOW] = cache[s]
            ref[s, WINDOW:WINDOW + DEC_SEQ] = new
            ref[s, WINDOW + DEC_SEQ:] = jnp.zeros((WINDOW - DEC_SEQ, A_KV_DIM), _f32)
        problems.append(([q[rows] for q in q_all], kk_ref[s], vv_ref[s], [z[rows] for z in z_all]))
    g_parts = _attn_core_batch(problems, bias_ref[...], sinks_ref)
    for c in range(A_Q_DIM // LANES):
        g_ref[step_rows, c * LANES:(c + 1) * LANES] = jnp.concatenate([p[c] for p in g_parts], axis=0).astype(g_ref.dtype)

    @pl.when(step == pl.num_programs(0) - 1)
    def _():
        _project_sample_rows(final, g_ref, w_ref, x_ref, gain_ref, res_refs)


def _sample_bias():
    t = jnp.arange(DEC_SEQ)[:, None]
    c = jnp.arange(2 * WINDOW)[None, :]
    ok = (c > t) & (c <= t + WINDOW)
    return jnp.where(ok, 0.0, NEG_BIG).astype(_f32)


def _attn_sample(qk, vz, sinks, cache_k, cache_v, layer, w_out, x, x_block, gain, into=None, kv_out=None):
    srows = SAMPLE_SEQS * DEC_SEQ
    row0 = P_ROWS // srows
    proj_in_specs, res_specs, res_shapes = _sample_projection_specs(layer, x_block, into)
    aliased = ([] if into is None else list(into)) + ([] if kv_out is None else list(kv_out))
    n_in = 9
    aliases = {} if into is None else {n_in + k: k for k in range(len(into))}
    if kv_out is not None:
        aliases.update({n_in + len(aliased) - 2 + k: len(res_specs) + k for k in range(2)})
    kv_spec = pl.BlockSpec((None, SAMPLE_SEQS, WINDOW, A_KV_DIM), lambda b: (layer, b, 0, 0))
    kv_shape = jax.ShapeDtypeStruct((N_A_LAYERS, DEC_BATCH, WINDOW, A_KV_DIM), _f32)
    outs = pl.pallas_call(
        functools.partial(_attn_sample_kernel, into is None, len(aliased)),
        grid=(DEC_BATCH // SAMPLE_SEQS,),
        in_specs=[
            pl.BlockSpec(memory_space=pltpu.SMEM),
            pl.BlockSpec((srows, A_HALF_DIM), lambda b: (row0 + b, 0)),
            pl.BlockSpec((srows, A_HALF_DIM), lambda b: (row0 + b, 0)),
            pl.BlockSpec((DEC_SEQ, 2 * WINDOW), lambda b: (0, 0)),
            kv_spec, kv_spec,
        ] + proj_in_specs + [pl.BlockSpec(memory_space=pl.ANY)] * len(aliased),
        out_specs=res_specs + [kv_spec, kv_spec],
        out_shape=res_shapes + [kv_shape, kv_shape],
        scratch_shapes=[pltpu.VMEM((SAMPLE_SEQS, 2 * WINDOW, A_KV_DIM), _f32)] * 2
                       + [pltpu.VMEM((S_ROWS, A_Q_DIM), _bf16)],
        input_output_aliases=aliases,
        compiler_params=_cparams(1),
        name="attn_sample",
    )(sinks, qk, vz, _sample_bias(), cache_k, cache_v, w_out, x, gain, *aliased)
    return outs[:-2], outs[-2], outs[-1]


CONV_COLS = 256


def _conv_taps(u, u1, u2, w_ref):
    return w_ref[0:1, :] * u2 + w_ref[1:2, :] * u1 + w_ref[2:3, :] * u


CONV_ROW_PARTS = 3
CONV_PART = MM_ROW_TILE // CONV_ROW_PARTS
CONV_TAIL_ROWS = [b * SEQ + SEQ - SUBLANES for b in range(BATCH)]
SAMPLE_ROW0_IN_TILE = P_ROWS - (ROWS // MM_ROW_TILE - 1) * MM_ROW_TILE


def _conv_mixer_kernel(h_ref, w_bg_ref, w_cg_ref, w_x_ref, w_z_ref, cw_ref, p1_ref, p2_ref,
                       g_ref, tail_ref, us_ref, wcat_ref, carry_ref):
    m = pl.program_id(1)

    @pl.when(m == 0)
    def _():
        for k, w_ref in enumerate((w_bg_ref, w_cg_ref, w_x_ref, w_z_ref)):
            wcat_ref[:, k * CONV_COLS:(k + 1) * CONV_COLS] = w_ref[...].astype(_bf16)
        carry_ref[...] = jnp.zeros_like(carry_ref)

    prev2 = carry_ref[SUBLANES - 2:SUBLANES - 1, :]
    prev1 = carry_ref[SUBLANES - 1:SUBLANES, :]
    for r in range(CONV_ROW_PARTS):
        rows = slice(r * CONV_PART, (r + 1) * CONV_PART)
        y = jnp.dot(h_ref[rows, :], wcat_ref[...], preferred_element_type=_f32)
        bg, cg, xi, z = [y[:, k * CONV_COLS:(k + 1) * CONV_COLS] for k in range(4)]
        u = cg * xi
        local = lax.broadcasted_iota(jnp.int32, u.shape, 0)
        row = m * MM_ROW_TILE + r * CONV_PART + local
        is_sample = row >= P_ROWS
        step = jnp.where(is_sample, row % DEC_SEQ, row % SEQ)
        p1 = p2 = 0.0
        lo = max(SAMPLE_ROW0_IN_TILE, r * CONV_PART) - r * CONV_PART
        if lo < CONV_PART:
            head = jnp.zeros((lo, CONV_COLS), _f32)
            past = slice(lo + r * CONV_PART - SAMPLE_ROW0_IN_TILE, None)
            p1 = jnp.where(is_sample, jnp.concatenate([head, p1_ref[past, :]], axis=0), 0.0)
            p2 = jnp.where(is_sample, jnp.concatenate([head, p2_ref[past, :]], axis=0), 0.0)
        u1 = jnp.where(local == 0, prev1, pltpu.roll(u, 1, 0))
        u2 = jnp.where(local == 0, prev2, jnp.where(local == 1, prev1, pltpu.roll(u, 2, 0)))
        u1 = jnp.where(step == 0, p1, u1)
        u2 = jnp.where(step <= 1, p2, u2)
        conv = _conv_taps(u, u1, u2, cw_ref)
        g_ref[rows, :] = (_silu(z) * bg * conv).astype(g_ref.dtype)
        prev2, prev1 = u[CONV_PART - 2:CONV_PART - 1, :], u[CONV_PART - 1:CONV_PART, :]
        for b, tail_row in enumerate(CONV_TAIL_ROWS):
            off = tail_row % MM_ROW_TILE - r * CONV_PART
            if 0 <= off < CONV_PART:
                hit = jnp.broadcast_to(m == tail_row // MM_ROW_TILE, (SUBLANES, CONV_COLS))
                pltpu.store(tail_ref.at[b], u[off:off + SUBLANES], mask=hit)
        lo = max(SAMPLE_ROW0_IN_TILE, r * CONV_PART)
        hi = (r + 1) * CONV_PART
        if lo < hi:
            us_ref[lo - SAMPLE_ROW0_IN_TILE:hi - SAMPLE_ROW0_IN_TILE, :] = u[lo - r * CONV_PART:]
    carry_ref[...] = u[CONV_PART - SUBLANES:]


def _conv_mixer(h, w, wj, conv_w, p1, p2):
    ncb = D_MODEL // CONV_COLS

    def wseg(k):
        return pl.BlockSpec((None, D_MODEL, CONV_COLS), lambda c, m: (wj, 0, k * ncb + c))

    tile_cols = pl.BlockSpec((S_ROWS, CONV_COLS), lambda c, m: (0, c))
    return pl.pallas_call(
        _conv_mixer_kernel,
        grid=(ncb, ROWS // MM_ROW_TILE),
        in_specs=[pl.BlockSpec((MM_ROW_TILE, D_MODEL), lambda c, m: (m, 0)),
                  wseg(0), wseg(1), wseg(2), wseg(3),
                  pl.BlockSpec((CONV_WIDTH, CONV_COLS), lambda c, m: (0, c)), tile_cols, tile_cols],
        out_specs=[pl.BlockSpec((MM_ROW_TILE, CONV_COLS), lambda c, m: (m, c)),
                   pl.BlockSpec((BATCH, SUBLANES, CONV_COLS), lambda c, m: (0, 0, c)),
                   pl.BlockSpec((S_ROWS, CONV_COLS), lambda c, m: (0, c))],
        out_shape=[jax.ShapeDtypeStruct((ROWS, D_MODEL), _bf16),
                   jax.ShapeDtypeStruct((BATCH, SUBLANES, D_MODEL), _f32),
                   jax.ShapeDtypeStruct((S_ROWS, D_MODEL), _f32)],
        scratch_shapes=[pltpu.VMEM((D_MODEL, 4 * CONV_COLS), _bf16), pltpu.VMEM((SUBLANES, CONV_COLS), _f32)],
        compiler_params=_cparams(2),
        name="conv_mixer",
    )(h, w, w, w, w, conv_w, p1, p2)


def _conv_past_rows(state):
    pad_to_seq = lambda a: jnp.pad(a, ((0, 0), (0, DEC_SEQ - a.shape[1]), (0, 0))).reshape(S_ROWS, D_MODEL)
    return pad_to_seq(state[:, 1:2]), pad_to_seq(state)


HG_ROWS = 256
HG_INTER_LAG = 1


def _lower_bound(logit_ref, layer):
    lg = logit_ref[...]
    e = jnp.exp(lg - jnp.max(lg, axis=0, keepdims=True))
    return jnp.sum(e[1:layer + 1], axis=0, keepdims=True) / jnp.sum(e, axis=0, keepdims=True)


def _hgrn_gates(q, fl, lb):
    f = lb + (1.0 - lb) * jax.nn.sigmoid(fl)
    return _silu(q), 1.0 - f, jnp.log2(f)


def _chunk_cumsum(x, chunk):
    t = lax.broadcasted_iota(jnp.int32, x.shape, 0) % chunk
    s = 1
    while s < chunk:
        x = x + jnp.where(t >= s, pltpu.roll(x, s, 0), 0.0)
        s *= 2
    return x


def _hgrn_out(o, z, ng):
    ms = jnp.mean(o * o, axis=-1, keepdims=True)
    return _silu(z) * (o * lax.rsqrt(ms + RMS_EPS) * ng)


def _hgrn_prompt_prep(lb, q, fl, v):
    n_chunks = HG_ROWS // C_CHUNK
    qq, kk, lf = _hgrn_gates(q, fl, lb)
    b = _chunk_cumsum(lf, C_CHUNK)
    b_last = [b[(c + 1) * C_CHUNK - 1:(c + 1) * C_CHUNK] for c in range(n_chunks)]
    b_end = jnp.concatenate([jnp.broadcast_to(r, (C_CHUNK, C_KEY_DIM)) for r in b_last], axis=0)
    q_t = (qq * jnp.exp2(b)).astype(_bf16)
    k_intra = (kk * jnp.exp2(-b)).astype(_bf16)
    k_state = (kk * jnp.exp2(b_end - b)).astype(_bf16)
    return q_t, k_intra, k_state, v.astype(_bf16), b_last


def _hgrn_stage_scores(q_t, k_intra):
    nt = (((1,), (1,)), ((), ()))
    return lax.dot_general(q_t, k_intra, nt, preferred_element_type=_f32)


def _hgrn_stage_intra(scores, k_state, vb):
    tn = (((0,), (0,)), ((), ()))
    row = lax.broadcasted_iota(jnp.int32, (HG_ROWS, HG_ROWS), 0)
    col = lax.broadcasted_iota(jnp.int32, (HG_ROWS, HG_ROWS), 1)
    keep = (col <= row) & (col >= row - row % C_CHUNK)
    att = jnp.where(keep, scores, 0.0)
    o_intra = jnp.dot(att.astype(_bf16), vb, preferred_element_type=_f32)
    sl = [slice(c * C_CHUNK, (c + 1) * C_CHUNK) for c in range(HG_ROWS // C_CHUNK)]
    upd = [lax.dot_general(vb[r], k_state[r], tn, preferred_element_type=_f32) for r in sl]
    return o_intra, upd


def _hgrn_stage_inter(q_t, b_last, upd, st):
    nt = (((1,), (1,)), ((), ()))
    n_chunks = HG_ROWS // C_CHUNK
    sl = [slice(c * C_CHUNK, (c + 1) * C_CHUNK) for c in range(n_chunks)]
    starts = []
    for c in range(n_chunks):
        starts.append(st.astype(_bf16))
        st = st * jnp.exp2(b_last[c]) + upd[c]
    o_inter = jnp.concatenate([lax.dot_general(q_t[sl[c]], starts[c], nt, preferred_element_type=_f32)
                               for c in range(n_chunks)], axis=0)
    return o_inter, st


def _hgrn_prompt_kernel(layer, q_ref, f_ref, i_ref, z_ref, lg_ref, ng_ref, w_ref, x_ref, gain_ref,
                        xn_ref, h_ref, so_ref, st_ref, g_ref, gdone_ref, wb_ref, y_ref):
    s = pl.program_id(0)
    n_seq_tiles = SEQ // HG_ROWS
    tt = s % n_seq_tiles

    @pl.when(s == 0)
    def _():
        wb_ref[...] = w_ref[...].astype(_bf16)
        gdone_ref[...] = jnp.zeros_like(gdone_ref)

    @pl.when(tt == 0)
    def _():
        st_ref[...] = jnp.zeros_like(st_ref)

    def finish():
        xn = x_ref[...] + y_ref[...]
        xn_ref[...] = xn
        h_ref[...] = _rms_scale(xn, gain_ref[...]).astype(h_ref.dtype)

    @pl.when(s == pl.num_programs(0) - 1)
    def _():
        y_ref[...] = jnp.dot(gdone_ref[...], wb_ref[...], preferred_element_type=_f32)
        finish()

    @pl.when(s < pl.num_programs(0) - 1)
    def _():
        lb_all = _lower_bound(lg_ref, layer)
        lanes = lambda hh: slice(hh * LANES, (hh + 1) * LANES)
        heads_per_group = C_HEADS // (D_MODEL // MXU_COLS)
        prepped, scored, intra = {}, {}, {}
        for t in range(C_HEADS + HG_INTER_LAG):
            if t < C_HEADS:
                q, fl, v = [r[:, lanes(t)].astype(_f32) for r in (q_ref, f_ref, i_ref)]
                prepped[t] = _hgrn_prompt_prep(lb_all[:, lanes(t)], q, fl, v)
                scored[t] = _hgrn_stage_scores(prepped[t][0], prepped[t][1])
            h1 = t - 1
            if 0 <= h1 < C_HEADS:
                _, _, k_state, vb, _ = prepped[h1]
                intra[h1] = _hgrn_stage_intra(scored.pop(h1), k_state, vb)
            h2 = t - HG_INTER_LAG
            if 0 <= h2 < C_HEADS:
                q_t, _, _, _, b_last = prepped.pop(h2)
                o_intra, upd = intra.pop(h2)
                o_inter, st = _hgrn_stage_inter(q_t, b_last, upd, st_ref[h2])
                st_ref[h2] = st
                g_ref[:, lanes(h2)] = _hgrn_out(o_inter + o_intra, z_ref[:, lanes(h2)].astype(_f32),
                                                ng_ref[...]).astype(g_ref.dtype)
            if t < C_HEADS and t % heads_per_group == heads_per_group - 1:
                cols = slice(t // heads_per_group * MXU_COLS, (t // heads_per_group + 1) * MXU_COLS)
                y_ref[:, cols] = jnp.dot(gdone_ref[...], wb_ref[:, cols], preferred_element_type=_f32)
        gdone_ref[...] = g_ref[...]
        finish()

        @pl.when(tt == n_seq_tiles - 1)
        def _():
            for hh in range(C_HEADS):
                so_ref[0, hh] = st_ref[hh].T


def _hgrn_prompt(proj, lb_logits, norm_g, layer, w_out, wj, x, gain):
    nt = SEQ // HG_ROWS
    n_tiles = BATCH * nt
    mix_tile = lambda s: jnp.minimum(s, n_tiles - 1)
    proj_tile = lambda width: pl.BlockSpec((HG_ROWS, width), lambda s: (jnp.maximum(s - 1, 0), 0))
    seg = lambda k: pl.BlockSpec((HG_ROWS, D_MODEL), lambda s: (mix_tile(s), k))
    g_tile = pltpu.VMEM((HG_ROWS, D_MODEL), _bf16)
    xn, h, states = pl.pallas_call(
        functools.partial(_hgrn_prompt_kernel, layer),
        grid=(n_tiles + 1,),
        in_specs=[seg(0), seg(1), seg(2), seg(3),
                  pl.BlockSpec((DEPTH, C_F_DIM), lambda s: (0, 0)),
                  pl.BlockSpec((1, C_VAL_DIM), lambda s: (0, 0)),
                  pl.BlockSpec((None, D_MODEL, D_MODEL), lambda s: (wj, 0, 0), pipeline_mode=pl.Buffered(1)),
                  proj_tile(D_MODEL),
                  pl.BlockSpec((1, D_MODEL), lambda s: (0, 0))],
        out_specs=[proj_tile(D_MODEL), proj_tile(D_MODEL),
                   pl.BlockSpec((1, C_HEADS, C_KEY_DIM, C_VAL_DIM), lambda s: (mix_tile(s) // nt, 0, 0, 0))],
        out_shape=_xn_h_shapes() + [jax.ShapeDtypeStruct((BATCH, C_HEADS, C_KEY_DIM, C_VAL_DIM), _f32)],
        scratch_shapes=[pltpu.VMEM((C_HEADS, C_VAL_DIM, C_KEY_DIM), _f32), g_tile, g_tile,
                        pltpu.VMEM((D_MODEL, D_MODEL), _bf16), pltpu.VMEM((HG_ROWS, D_MODEL), _f32)],
        compiler_params=_cparams(1),
        name="hgrn_prompt",
    )(proj, proj, proj, proj, lb_logits, norm_g, w_out, x, gain)
    return (xn, h), states


def _hgrn_sample_kernel(layer, proj_ref, lg_ref, ng_ref, s_ref, w_ref, x_ref, gain_ref, xn_any, h_any,
                        xn_ref, h_ref, so_ref, g_ref):
    del xn_any, h_any
    step = pl.program_id(0)
    srows = SAMPLE_SEQS * DEC_SEQ
    step_rows = pl.ds(pl.multiple_of(step * srows, srows), srows)
    nt = (((1,), (1,)), ((), ()))
    tn = (((0,), (0,)), ((), ()))
    row = lax.broadcasted_iota(jnp.int32, (DEC_SEQ, DEC_SEQ), 0)
    col = lax.broadcasted_iota(jnp.int32, (DEC_SEQ, DEC_SEQ), 1)
    causal = row >= col
    eye = (lax.broadcasted_iota(jnp.int32, (C_KEY_DIM, C_KEY_DIM), 0)
           == lax.broadcasted_iota(jnp.int32, (C_KEY_DIM, C_KEY_DIM), 1))
    for h in range(C_HEADS):
        hs = slice(h * LANES, (h + 1) * LANES)
        lb = _lower_bound(lg_ref.at[:, hs], layer)
        q, fl, v, z = [proj_ref[:, off + h * LANES:off + (h + 1) * LANES].astype(_f32)
                       for off in (0, C_F_DIM, 2 * C_F_DIM, 2 * C_F_DIM + D_MODEL)]
        qq, kk, lf = _hgrn_gates(q, fl, lb)
        b = _chunk_cumsum(lf, DEC_SEQ)
        q_t = qq * jnp.exp2(b)
        k_intra = kk * jnp.exp2(-b)
        outs = []
        for s in range(SAMPLE_SEQS):
            rows = slice(s * DEC_SEQ, (s + 1) * DEC_SEQ)
            b_last = b[(s + 1) * DEC_SEQ - 1:(s + 1) * DEC_SEQ]
            qb = q_t[rows].astype(_bf16)
            k_state = (kk[rows] * jnp.exp2(b_last - b[rows])).astype(_bf16)
            vb = v[rows].astype(_bf16)
            s0 = s_ref[s, h]
            o_inter = jnp.dot(qb, s0.astype(_bf16), preferred_element_type=_f32)
            att = jnp.where(causal, lax.dot_general(qb, k_intra[rows].astype(_bf16), nt,
                                                    preferred_element_type=_f32), 0.0)
            outs.append(o_inter + jnp.dot(att.astype(_bf16), vb, preferred_element_type=_f32))
            decay = jnp.sum(jnp.where(eye, jnp.broadcast_to(jnp.exp2(b_last), eye.shape), 0.0),
                            axis=1, keepdims=True)
            so_ref[s, h] = decay * s0 + lax.dot_general(k_state, vb, tn, preferred_element_type=_f32)
        g_ref[step_rows, hs] = _hgrn_out(jnp.concatenate(outs, axis=0), z, ng_ref[...]).astype(g_ref.dtype)

    @pl.when(step == pl.num_programs(0) - 1)
    def _():
        _project_sample_rows(False, g_ref, w_ref, x_ref, gain_ref, (xn_ref, h_ref))


def _hgrn_sample(proj, lb_logits, norm_g, state, layer, w_out, wj, x, gain, into):
    srows = SAMPLE_SEQS * DEC_SEQ
    row0 = P_ROWS // srows
    proj_in_specs, res_specs, res_shapes = _sample_projection_specs(wj, P_ROWS // S_ROWS, into)
    n_in = 7
    state_spec = pl.BlockSpec((SAMPLE_SEQS, C_HEADS, C_KEY_DIM, C_VAL_DIM), lambda b: (b, 0, 0, 0))
    outs = pl.pallas_call(
        functools.partial(_hgrn_sample_kernel, layer),
        grid=(DEC_BATCH // SAMPLE_SEQS,),
        in_specs=[pl.BlockSpec((srows, C_IN_DIM), lambda b: (row0 + b, 0)),
                  pl.BlockSpec((DEPTH, C_F_DIM), lambda b: (0, 0)),
                  pl.BlockSpec((1, C_VAL_DIM), lambda b: (0, 0)),
                  state_spec] + proj_in_specs + [pl.BlockSpec(memory_space=pl.ANY)] * len(into),
        out_specs=res_specs + [state_spec],
        out_shape=res_shapes + [jax.ShapeDtypeStruct((DEC_BATCH, C_HEADS, C_KEY_DIM, C_VAL_DIM), _f32)],
        scratch_shapes=[pltpu.VMEM((S_ROWS, D_MODEL), _bf16)],
        input_output_aliases={n_in + k: k for k in range(len(into))},
        compiler_params=_cparams(1),
        name="hgrn_sample",
    )(proj, lb_logits, norm_g, state, w_out, x, gain, *into)
    return outs[:-1], outs[-1]


def kernel(x_prompt, x_sample, cache_k, cache_v, state_conv, state_hgrn, ln_g, final_g, a_w_in, a_w_out, a_sinks,
           b_w_in, b_conv_w, b_w_out, c_w_in, c_norm_g, c_w_out, c_lb_logits):
    xp = x_prompt.reshape(P_ROWS, D_MODEL)
    xs = x_sample.reshape(S_ROWS, D_MODEL)
    tabs = jnp.asarray(_rope_tables())
    norm_gains = [ln_g[li].reshape(1, D_MODEL) for li in range(1, DEPTH)] + [final_g.reshape(1, D_MODEL)]

    h = _first_norm(xp, xs, ln_g[0].reshape(1, D_MODEL))
    x = None
    ks_p, vs_p, conv_p, conv_s, hg_p, hg_s = [], [], [], [], [], []
    kv_sample = None
    caches = [c.reshape(N_A_LAYERS, DEC_BATCH, WINDOW, A_KV_DIM) for c in (cache_k, cache_v)]
    for li in range(DEPTH):
        kind, j = li % 3, li // 3
        if kind == 0:
            qk = _in_proj(h, a_w_in, j, A_HALF_DIM, A_COL_TILE, rope_tabs=tabs)
            vz = _in_proj(h, a_w_in, j, A_HALF_DIM, A_COL_TILE, col0=A_HALF_DIM)
            final = li == DEPTH - 1
            prompt_res, k_p, v_p = _attn_prompt(qk, vz, a_sinks[j], a_w_out, j, xp if li == 0 else x, norm_gains[li],
                                                final)
            ks_p.append(k_p), vs_p.append(v_p)
            x_s, x_block = (xs, 0) if li == 0 else (x, P_ROWS // S_ROWS)
            res, *kv_sample = _attn_sample(qk, vz, a_sinks[j], caches[0], caches[1], j, a_w_out, x_s, x_block,
                                           norm_gains[li], None if final else prompt_res, kv_sample)
            if final:
                (y_p,), (y_s,) = prompt_res, res
            else:
                x, h = res
            continue
        elif kind == 1:
            g, tail, u_s = _conv_mixer(h, b_w_in, j, b_conv_w[j], *_conv_past_rows(state_conv[j]))
            conv_p.append(tail[:, SUBLANES - (CONV_WIDTH - 1):])
            conv_s.append(u_s.reshape(DEC_BATCH, DEC_SEQ, D_MODEL)[:, DEC_SEQ - (CONV_WIDTH - 1):])
            w_out = b_w_out
        else:
            proj = _in_proj(h, c_w_in, j, C_IN_DIM, MM_COL_TILE)
            ng = c_norm_g[j].reshape(1, C_VAL_DIM)
            res, s_p = _hgrn_prompt(proj, c_lb_logits, ng, li, c_w_out, j, x, norm_gains[li])
            (x, h), s_s = _hgrn_sample(proj, c_lb_logits, ng, state_hgrn[j], li, c_w_out, j, x, norm_gains[li], res)
            hg_p.append(s_p), hg_s.append(s_s)
            continue
        x, h = _out_proj(g, w_out, j, x, norm_gains[li])

    def stacked(parts, *shape):
        arr = parts[0][None] if len(parts) == 1 else jnp.stack(parts)
        return arr.reshape(len(parts), *shape)

    kv_shape = (WINDOW, A_KV_HEADS, A_HEAD_DIM)
    return (y_p.reshape(BATCH, SEQ, D_MODEL), y_s.reshape(DEC_BATCH, DEC_SEQ, D_MODEL),
            stacked(ks_p, BATCH, *kv_shape), stacked(vs_p, BATCH, *kv_shape),
            kv_sample[0].reshape(N_A_LAYERS, DEC_BATCH, *kv_shape),
            kv_sample[1].reshape(N_A_LAYERS, DEC_BATCH, *kv_shape),
            stacked(conv_p, BATCH, CONV_WIDTH - 1, D_MODEL), stacked(conv_s, DEC_BATCH, CONV_WIDTH - 1, D_MODEL),
            stacked(hg_p, BATCH, C_HEADS, C_KEY_DIM, C_VAL_DIM),
            stacked(hg_s, DEC_BATCH, C_HEADS, C_KEY_DIM, C_VAL_DIM))
```

```python
import functools

import numpy as np

import jax
import jax.numpy as jnp
from jax import lax
from jax.experimental import pallas as pl
from jax.experimental.pallas import tpu as pltpu

D_MODEL = 2048
BATCH = 4
SEQ = 2048
DEPTH = 4
N_A_LAYERS = (DEPTH + 2) // 3
DEC_BATCH = 32
DEC_SEQ = 8
PAST_LEN = 16384
RMS_EPS = 1e-6

A_HEADS = 32
A_KV_HEADS = 4
A_HEAD_DIM = 64
A_GROUP = A_HEADS // A_KV_HEADS
A_Q_DIM = A_HEADS * A_HEAD_DIM
A_KV_DIM = A_KV_HEADS * A_HEAD_DIM
WINDOW = 128
ROT_DIM = A_HEAD_DIM // 4
ROPE_THETA = 500000.0

CONV_WIDTH = 3

C_HEADS = 16
C_KEY_DIM = 128
C_VAL_DIM = D_MODEL // C_HEADS
C_F_DIM = C_HEADS * C_KEY_DIM
C_IN_DIM = 2 * C_F_DIM + 2 * D_MODEL
C_CHUNK = 32

LANES = 128
SUBLANES = 8
MXU_COLS = 256
P_ROWS = BATCH * SEQ
S_ROWS = DEC_BATCH * DEC_SEQ
ROWS = P_ROWS + S_ROWS
ROW_TILE = 384
SAMPLE_SEQS = 4
MM_ROW_TILE = ROWS // 4
MM_COL_TILE = 1024
MM_ROW_PARTS = 6
NEG_BIG = -1e30
LOG2E = 1.4426950408889634
Q_SCALE = A_HEAD_DIM ** -0.5 * LOG2E
VMEM_LIMIT = 56 * 1024 * 1024

_f32 = jnp.float32
_bf16 = jnp.bfloat16


def _cparams(n_axes):
    return pltpu.CompilerParams(dimension_semantics=("arbitrary",) * n_axes, vmem_limit_bytes=VMEM_LIMIT)


def _silu(x):
    return x * jax.nn.sigmoid(x)


def _rms_scale(x, g):
    ms = jnp.mean(x * x, axis=-1, keepdims=True)
    return x * lax.rsqrt(ms + RMS_EPS) * g


NORM_ROW_TILE = 1024


def _first_norm_kernel(x_ref, g_ref, h_ref):
    h_ref[...] = _rms_scale(x_ref[...], g_ref[...]).astype(h_ref.dtype)


def _first_norm_sample_kernel(x_ref, g_ref, hin_ref, h_ref):
    del hin_ref
    h_ref[...] = _rms_scale(x_ref[...], g_ref[...]).astype(h_ref.dtype)


def _first_norm(xp, xs, g):
    gain_spec = pl.BlockSpec((1, D_MODEL), lambda m: (0, 0))
    h = pl.pallas_call(
        _first_norm_kernel,
        grid=(P_ROWS // NORM_ROW_TILE,),
        in_specs=[pl.BlockSpec((NORM_ROW_TILE, D_MODEL), lambda m: (m, 0)), gain_spec],
        out_specs=pl.BlockSpec((NORM_ROW_TILE, D_MODEL), lambda m: (m, 0)),
        out_shape=jax.ShapeDtypeStruct((ROWS, D_MODEL), _bf16),
        compiler_params=_cparams(1),
        name="first_norm",
    )(xp, g)
    return pl.pallas_call(
        _first_norm_sample_kernel,
        grid=(1,),
        in_specs=[pl.BlockSpec((S_ROWS, D_MODEL), lambda m: (0, 0)), gain_spec, pl.BlockSpec(memory_space=pl.ANY)],
        out_specs=pl.BlockSpec((S_ROWS, D_MODEL), lambda m: (P_ROWS // S_ROWS, 0)),
        out_shape=jax.ShapeDtypeStruct((ROWS, D_MODEL), _bf16),
        input_output_aliases={2: 0},
        compiler_params=_cparams(1),
        name="first_norm_sample",
    )(xs, g, h)


def _in_proj_kernel(h_ref, w_ref, o_ref, wb_ref):
    @pl.when(pl.program_id(1) == 0)
    def _():
        wb_ref[...] = w_ref[...].astype(_bf16)

    part = MM_ROW_TILE // MM_ROW_PARTS
    for r in range(MM_ROW_PARTS):
        rows = slice(r * part, (r + 1) * part)
        o_ref[rows, :] = jnp.dot(h_ref[rows, :], wb_ref[...], preferred_element_type=_f32).astype(o_ref.dtype)


def _in_proj(h, w, wj, n_out, tn):
    return pl.pallas_call(
        _in_proj_kernel,
        grid=(n_out // tn, ROWS // MM_ROW_TILE),
        in_specs=[pl.BlockSpec((MM_ROW_TILE, D_MODEL), lambda j, m: (m, 0)),
                  pl.BlockSpec((None, D_MODEL, tn), lambda j, m: (wj, 0, j))],
        out_specs=pl.BlockSpec((MM_ROW_TILE, tn), lambda j, m: (m, j)),
        out_shape=jax.ShapeDtypeStruct((ROWS, n_out), _bf16),
        scratch_shapes=[pltpu.VMEM((D_MODEL, tn), _bf16)],
        compiler_params=_cparams(2),
        name="in_proj",
    )(h, w)


A_ROW_TILE = ROWS // 8
A_ROW_PARTS = 6


def _attn_in_proj_kernel(rope, h_ref, w_ref, *refs):
    tab_ref = refs[0] if rope else None
    o_ref, wb_ref = refs[-2:]
    s = pl.program_id(0)
    n_col_tiles = A_HALF_DIM // A_COL_TILE
    part = A_ROW_TILE // A_ROW_PARTS
    half = ROT_DIM // 2

    def project(col0, width):
        for r in range(A_ROW_PARTS):
            rows = slice(r * part, (r + 1) * part)
            y = jnp.dot(h_ref[rows, :], wb_ref[:, col0:col0 + width], preferred_element_type=_f32)
            if not rope:
                o_ref[rows, col0:col0 + width] = y.astype(o_ref.dtype)
                continue
            tabs = [tab_ref[k, rows, :] for k in range(3)]
            for c in range(width // LANES):
                col = col0 + c * LANES
                yc = y[:, c * LANES:(c + 1) * LANES]
                rc = yc * tabs[0] + pltpu.roll(yc, LANES - half, 1) * tabs[1] + pltpu.roll(yc, half, 1) * tabs[2]
                if col < A_Q_DIM:
                    rc = rc * Q_SCALE
                o_ref[rows, col:col + LANES] = rc.astype(o_ref.dtype)

    for c in range(n_col_tiles):
        @pl.when(s == c)
        def _():
            wb_ref[:, c * A_COL_TILE:(c + 1) * A_COL_TILE] = w_ref[...].astype(_bf16)
            project(c * A_COL_TILE, A_COL_TILE)

    @pl.when(s >= n_col_tiles)
    def _():
        project(0, A_HALF_DIM)


def _attn_in_proj(h, w, wj, col0, rope_tabs=None):
    n_col_tiles = A_HALF_DIM // A_COL_TILE
    cb0 = col0 // A_COL_TILE
    row_tile = lambda s: jnp.maximum(s - (n_col_tiles - 1), 0)
    in_specs = [pl.BlockSpec((A_ROW_TILE, D_MODEL), lambda s: (row_tile(s), 0)),
                pl.BlockSpec((None, D_MODEL, A_COL_TILE), lambda s: (wj, 0, cb0 + jnp.minimum(s, n_col_tiles - 1)))]
    args = [h, w]
    if rope_tabs is not None:
        in_specs.append(pl.BlockSpec((3, A_ROW_TILE, LANES), lambda s: (0, row_tile(s), 0)))
        args.append(rope_tabs)
    return pl.pallas_call(
        functools.partial(_attn_in_proj_kernel, rope_tabs is not None),
        grid=(ROWS // A_ROW_TILE + n_col_tiles - 1,),
        in_specs=in_specs,
        out_specs=pl.BlockSpec((A_ROW_TILE, A_HALF_DIM), lambda s: (row_tile(s), 0)),
        out_shape=jax.ShapeDtypeStruct((ROWS, A_HALF_DIM), _bf16),
        scratch_shapes=[pltpu.VMEM((D_MODEL, A_HALF_DIM), _bf16)],
        compiler_params=_cparams(1),
        name="attn_in_proj_rope" if rope_tabs is not None else "attn_in_proj",
    )(*args)


def _out_proj_kernel(a_ref, w_ref, x_ref, g_ref, xn_ref, h_ref, wb_ref):
    @pl.when(pl.program_id(0) == 0)
    def _():
        wb_ref[...] = w_ref[...].astype(_bf16)

    xn = x_ref[...] + jnp.dot(a_ref[...], wb_ref[...], preferred_element_type=_f32)
    xn_ref[...] = xn
    h_ref[...] = _rms_scale(xn, g_ref[...]).astype(h_ref.dtype)


def _xn_h_shapes():
    return [jax.ShapeDtypeStruct((ROWS, D_MODEL), _f32), jax.ShapeDtypeStruct((ROWS, D_MODEL), _bf16)]


def _out_proj(a, w, wj, x, g):
    tile = pl.BlockSpec((ROW_TILE, D_MODEL), lambda m: (m, 0))
    return pl.pallas_call(
        _out_proj_kernel,
        grid=(ROWS // ROW_TILE,),
        in_specs=[tile,
                  pl.BlockSpec((None, D_MODEL, D_MODEL), lambda m: (wj, 0, 0), pipeline_mode=pl.Buffered(1)),
                  tile, pl.BlockSpec((1, D_MODEL), lambda m: (0, 0))],
        out_specs=[tile, tile],
        out_shape=_xn_h_shapes(),
        scratch_shapes=[pltpu.VMEM((D_MODEL, D_MODEL), _bf16)],
        compiler_params=_cparams(1),
        name="out_proj",
    )(a, w, x, g)


def _rope_tables():
    pos = np.concatenate([np.arange(P_ROWS) % SEQ, PAST_LEN + np.arange(S_ROWS) % DEC_SEQ]).astype(np.float64)
    half = ROT_DIM // 2
    inv = ROPE_THETA ** (-np.arange(half, dtype=np.float64) * 2.0 / ROT_DIM)
    ang = pos[:, None] * inv[None, :]
    cos, sin = np.cos(ang), np.sin(ang)
    rest = A_HEAD_DIM - ROT_DIM
    c_head = np.concatenate([cos, cos, np.ones((ROWS, rest))], axis=1)
    s1_head = np.concatenate([-sin, np.zeros((ROWS, half + rest))], axis=1)
    s2_head = np.concatenate([np.zeros((ROWS, half)), sin, np.zeros((ROWS, rest))], axis=1)
    rep = LANES // A_HEAD_DIM
    return np.stack([np.tile(t, (1, rep)) for t in (c_head, s1_head, s2_head)]).astype(np.float32)


def _lane_halves(xc, head_in_low_half):
    lane = lax.broadcasted_iota(jnp.int32, xc.shape, 1)
    low = lane < A_HEAD_DIM
    sw = pltpu.roll(xc, A_HEAD_DIM, 1)
    if head_in_low_half:
        return jnp.where(low, xc, 0.0), jnp.where(low, 0.0, sw)
    return jnp.where(low, sw, 0.0), jnp.where(low, 0.0, xc)


def _attn_core_batch(problems, bias, sinks_ref):
    nt = (((1,), (1,)), ((), ()))
    cols_per_kv = A_GROUP * A_HEAD_DIM // LANES
    tq = problems[0][0][0].shape[0]
    lane = lax.broadcasted_iota(jnp.int32, (tq, LANES), 1)
    low = lane < A_HEAD_DIM
    pairs = [(n, j) for n in range(len(problems)) for j in range(A_KV_HEADS)]
    scores, values = {}, {}
    for n, j in pairs:
        q_cols, kk, vv, _ = problems[n]
        kc = kk[:, (j // 2) * LANES:(j // 2 + 1) * LANES]
        vc = vv[:, (j // 2) * LANES:(j // 2 + 1) * LANES]
        k_lo, k_hi = _lane_halves(kc, j % 2 == 0)
        values[n, j] = _lane_halves(vc, j % 2 == 0)
        qs = jnp.concatenate(q_cols[j * cols_per_kv:(j + 1) * cols_per_kv], axis=0).astype(_bf16)
        scores[n, j] = (lax.dot_general(qs, k_lo.astype(_bf16), nt, preferred_element_type=_f32),
                        lax.dot_general(qs, k_hi.astype(_bf16), nt, preferred_element_type=_f32))
    probs = {}
    for n, j in pairs:
        p_lo, p_hi, r_lo, r_hi = [], [], [], []
        for c in range(cols_per_kv):
            for s_all, ps, rs, odd in ((scores[n, j][0], p_lo, r_lo, 0), (scores[n, j][1], p_hi, r_hi, 1)):
                sink = sinks_ref[j * A_GROUP + 2 * c + odd] * LOG2E
                s = s_all[c * tq:(c + 1) * tq] + bias
                m = jnp.maximum(jnp.max(s, axis=-1, keepdims=True), sink)
                p = jnp.exp2(s - m)
                den = jnp.sum(p, axis=-1, keepdims=True) + jnp.exp2(sink - m)
                ps.append(p)
                rs.append(1.0 / den)
        probs[n, j] = (jnp.concatenate(p_lo, axis=0).astype(_bf16), jnp.concatenate(p_hi, axis=0).astype(_bf16),
                       r_lo, r_hi)
    outs = [[] for _ in problems]
    for n, j in pairs:
        p_lo, p_hi, r_lo, r_hi = probs[n, j]
        v_lo, v_hi = values[n, j]
        o = (jnp.dot(p_lo, v_lo.astype(_bf16), preferred_element_type=_f32)
             + jnp.dot(p_hi, v_hi.astype(_bf16), preferred_element_type=_f32))
        z_cols = problems[n][3]
        for c in range(cols_per_kv):
            oc = o[c * tq:(c + 1) * tq] * jnp.where(low, r_lo[c], r_hi[c])
            outs[n].append(_silu(z_cols[j * cols_per_kv + c]) * oc)
    return outs


Q_COLS_PER_KV = A_GROUP * A_HEAD_DIM // LANES


def _attn_stage_scores(q_cols, kk, j):
    nt = (((1,), (1,)), ((), ()))
    k_lo, k_hi = _lane_halves(kk[:, (j // 2) * LANES:(j // 2 + 1) * LANES], j % 2 == 0)
    qs = jnp.concatenate(q_cols, axis=0).astype(_bf16)
    return (lax.dot_general(k_lo.astype(_bf16), qs, nt, preferred_element_type=_f32),
            lax.dot_general(k_hi.astype(_bf16), qs, nt, preferred_element_type=_f32))


def _attn_stage_values(scores, vv, j, bias_t, sinks_ref):
    tq = WINDOW
    zeros = jnp.zeros((A_HEAD_DIM, 2 * WINDOW), _f32)
    srow = lax.broadcasted_iota(jnp.int32, (2 * SUBLANES, 4 * WINDOW), 0)
    scol = lax.broadcasted_iota(jnp.int32, (2 * SUBLANES, 4 * WINDOW), 1)
    sum_rows = jnp.where(srow == scol // (2 * WINDOW), 1.0, 0.0)
    p_lo, p_hi, r_lo, r_hi = [], [], [], []
    for c in range(Q_COLS_PER_KV):
        for st_all, ps, rs, odd in ((scores[0], p_lo, r_lo, 0), (scores[1], p_hi, r_hi, 1)):
            sink = sinks_ref[j * A_GROUP + 2 * c + odd] * LOG2E
            s = st_all[:, c * tq:(c + 1) * tq] + bias_t
            m = jnp.maximum(jnp.max(s, axis=0, keepdims=True), sink)
            ps.append(jnp.exp2(s - m).astype(_bf16))
            rs.append(jnp.exp2(sink - m))
    pt = jnp.concatenate([jnp.concatenate(p_lo, axis=1), jnp.concatenate(p_hi, axis=1)], axis=0)
    vt = vv[:, (j // 2) * LANES:(j // 2 + 1) * LANES].T
    vj = vt[:A_HEAD_DIM] if j % 2 == 0 else vt[A_HEAD_DIM:]
    vcat = jnp.concatenate([jnp.concatenate([vj, zeros], axis=1), jnp.concatenate([zeros, vj], axis=1),
                            sum_rows], axis=0)
    return jnp.dot(vcat.astype(_bf16), pt, preferred_element_type=_f32), (r_lo, r_hi)


def _attn_stage_output(ot, sink_terms, z_cols):
    tq = WINDOW
    low = lax.broadcasted_iota(jnp.int32, (LANES, tq), 0) < A_HEAD_DIM
    out = []
    for c in range(Q_COLS_PER_KV):
        cols = slice(c * tq, (c + 1) * tq)
        den_lo = ot[LANES:LANES + 1, cols] + sink_terms[0][c]
        den_hi = ot[LANES + 1:LANES + 2, cols] + sink_terms[1][c]
        oc = ot[:LANES, cols] * jnp.where(low, 1.0 / den_lo, 1.0 / den_hi)
        out.append(_silu(z_cols[c]) * oc.T)
    return out


def _lane_chunks(ref, start, width, rows=slice(None), dtype=None):
    chunks = [ref[rows, start + c * LANES:start + (c + 1) * LANES] for c in range(width // LANES)]
    return chunks if dtype is None else [c.astype(dtype) for c in chunks]


def _attn_prompt_kernel(final, sinks_ref, qk_ref, vz_ref, bias_ref, w_ref, x_ref, gain_ref, *refs):
    out_refs, (ko_ref, vo_ref, kprev, vprev, g_ref, gdone_ref, wb_ref, y_ref) = refs[:-8], refs[-8:]
    s = pl.program_id(0)
    i = s % (SEQ // (ATTN_BLOCKS_PER_STEP * WINDOW))

    @pl.when(s == 0)
    def _():
        wb_ref[...] = w_ref[...].astype(_bf16)
        gdone_ref[...] = jnp.zeros_like(gdone_ref)

    @pl.when(i == 0)
    def _():
        kprev[...] = jnp.zeros_like(kprev)
        vprev[...] = jnp.zeros_like(vprev)

    def finish():
        xn = x_ref[...] + y_ref[...]
        if final:
            out_refs[0][...] = _rms_scale(xn, gain_ref[...])
        else:
            out_refs[0][...] = xn
            out_refs[1][...] = _rms_scale(xn, gain_ref[...]).astype(out_refs[1].dtype)

    @pl.when(s == pl.num_programs(0) - 1)
    def _():
        y_ref[...] = jnp.dot(gdone_ref[...], wb_ref[...], preferred_element_type=_f32)
        finish()

    @pl.when(s < pl.num_programs(0) - 1)
    def _():
        blocks = []
        k_prev, v_prev = kprev[...], vprev[...]
        for sub in range(ATTN_BLOCKS_PER_STEP):
            rows = slice(sub * WINDOW, (sub + 1) * WINDOW)
            k_new = qk_ref[rows, A_Q_DIM:].astype(_f32)
            v_new = vz_ref[rows, :A_KV_DIM].astype(_f32)
            bias = jnp.where(i == 0, bias_ref[0], bias_ref[1]) if sub == 0 else bias_ref[1]
            blocks.append((rows, jnp.concatenate([k_prev, k_new], axis=0), jnp.concatenate([v_prev, v_new], axis=0), bias))
            k_prev, v_prev = k_new, v_new
        ko_ref[0] = k_prev
        vo_ref[0] = v_prev
        kprev[...] = k_prev
        vprev[...] = v_prev

        items = [(sub, j) for sub in range(ATTN_BLOCKS_PER_STEP) for j in range(A_KV_HEADS)]
        assert len(items) == D_MODEL // MXU_COLS
        scores, values = {}, {}
        for t in range(len(items) + 2):
            if t < len(items):
                sub, j = items[t]
                rows, kk, _, _ = blocks[sub]
                q_cols = _lane_chunks(qk_ref, j * Q_COLS_PER_KV * LANES, Q_COLS_PER_KV * LANES, rows)
                scores[t] = _attn_stage_scores(q_cols, kk, j)
                cols = slice(t * MXU_COLS, (t + 1) * MXU_COLS)
                y_ref[:, cols] = jnp.dot(gdone_ref[...], wb_ref[:, cols], preferred_element_type=_f32)
            if 0 <= t - 1 < len(items):
                sub, j = items[t - 1]
                _, _, vv, bias = blocks[sub]
                values[t - 1] = _attn_stage_values(scores.pop(t - 1), vv, j, bias, sinks_ref)
            if 0 <= t - 2 < len(items):
                sub, j = items[t - 2]
                rows = blocks[sub][0]
                first = j * Q_COLS_PER_KV * LANES
                z_cols = _lane_chunks(vz_ref, A_KV_DIM + first, Q_COLS_PER_KV * LANES, rows, dtype=_f32)
                for c, gc in enumerate(_attn_stage_output(*values.pop(t - 2), z_cols)):
                    g_ref[rows, first + c * LANES:first + (c + 1) * LANES] = gc.astype(g_ref.dtype)
        gdone_ref[...] = g_ref[...]
        finish()


def _prompt_bias_t():
    c = jnp.arange(2 * WINDOW)[:, None]
    r = jnp.arange(WINDOW)[None, :]
    band = (c > r) & (c <= r + WINDOW)
    first = band & (c >= WINDOW)
    return jnp.where(jnp.stack([first, band]), 0.0, NEG_BIG).astype(_f32)


A_HALF_DIM = A_Q_DIM + A_KV_DIM
A_COL_TILE = A_HALF_DIM // 3


ATTN_BLOCKS_PER_STEP = 2


def _attn_prompt(qk, vz, sinks, w_out, wj, x, gain, final):
    rows = ATTN_BLOCKS_PER_STEP * WINDOW
    nb = SEQ // rows
    n_tiles = BATCH * nb
    attn_tile = lambda width: pl.BlockSpec((rows, width), lambda s: (jnp.minimum(s, n_tiles - 1), 0))
    proj_tile = lambda width: pl.BlockSpec((rows, width), lambda s: (jnp.maximum(s - 1, 0), 0))
    kv_spec = pl.BlockSpec((1, WINDOW, A_KV_DIM), lambda s: (jnp.minimum(s, n_tiles - 1) // nb, 0, 0))
    kv_shape = jax.ShapeDtypeStruct((BATCH, WINDOW, A_KV_DIM), _f32)
    if final:
        res_specs, res_shapes = [proj_tile(D_MODEL)], [jax.ShapeDtypeStruct((P_ROWS, D_MODEL), _f32)]
    else:
        res_specs, res_shapes = [proj_tile(D_MODEL), proj_tile(D_MODEL)], _xn_h_shapes()
    g_tile = pltpu.VMEM((rows, A_Q_DIM), _bf16)
    outs = pl.pallas_call(
        functools.partial(_attn_prompt_kernel, final),
        grid=(n_tiles + 1,),
        in_specs=[
            pl.BlockSpec(memory_space=pltpu.SMEM),
            attn_tile(A_HALF_DIM), attn_tile(A_HALF_DIM),
            pl.BlockSpec((2, 2 * WINDOW, WINDOW), lambda s: (0, 0, 0)),
            pl.BlockSpec((None, D_MODEL, D_MODEL), lambda s: (wj, 0, 0), pipeline_mode=pl.Buffered(1)),
            proj_tile(D_MODEL),
            pl.BlockSpec((1, D_MODEL), lambda s: (0, 0)),
        ],
        out_specs=res_specs + [kv_spec, kv_spec],
        out_shape=res_shapes + [kv_shape, kv_shape],
        scratch_shapes=[pltpu.VMEM((WINDOW, A_KV_DIM), _f32), pltpu.VMEM((WINDOW, A_KV_DIM), _f32),
                        g_tile, g_tile, pltpu.VMEM((D_MODEL, D_MODEL), _bf16), pltpu.VMEM((rows, D_MODEL), _f32)],
        compiler_params=_cparams(1),
        name="attn_prompt",
    )(sinks, qk, vz, _prompt_bias_t(), w_out, x, gain)
    return outs[:-2], outs[-2], outs[-1]


def _project_sample_rows(final, g_ref, w_ref, x_ref, gain_ref, out_refs):
    g = g_ref[...]
    y = [jnp.dot(g, w_ref[:, c * MXU_COLS:(c + 1) * MXU_COLS].astype(_bf16), preferred_element_type=_f32)
         for c in range(D_MODEL // MXU_COLS)]
    xn = x_ref[...] + jnp.concatenate(y, axis=1)
    if final:
        out_refs[0][...] = _rms_scale(xn, gain_ref[...])
    else:
        out_refs[0][...] = xn
        out_refs[1][...] = _rms_scale(xn, gain_ref[...]).astype(out_refs[1].dtype)


def _sample_projection_specs(wj, x_block, into):
    once = lambda shape, index: pl.BlockSpec(shape, index, pipeline_mode=pl.Buffered(1))
    in_specs = [once((None, D_MODEL, D_MODEL), lambda b: (wj, 0, 0)),
                once((S_ROWS, D_MODEL), lambda b: (x_block, 0)),
                pl.BlockSpec((1, D_MODEL), lambda b: (0, 0))]
    if into is None:
        return in_specs, [pl.BlockSpec((S_ROWS, D_MODEL), lambda b: (0, 0))], [jax.ShapeDtypeStruct((S_ROWS, D_MODEL), _f32)]
    sample_block = pl.BlockSpec((S_ROWS, D_MODEL), lambda b: (P_ROWS // S_ROWS, 0))
    return in_specs, [sample_block, sample_block], _xn_h_shapes()


def _attn_sample_kernel(final, n_aliased, sinks_ref, qk_ref, vz_ref, bias_ref, ck_ref, cv_ref, w_ref, x_ref, gain_ref,
                        *refs):
    *res_refs, ko_ref, vo_ref, kk_ref, vv_ref, g_ref = refs[n_aliased:]
    step = pl.program_id(0)
    srows = SAMPLE_SEQS * DEC_SEQ
    step_rows = pl.ds(pl.multiple_of(step * srows, srows), srows)
    q_all = _lane_chunks(qk_ref, 0, A_Q_DIM, dtype=_f32)
    k_all = qk_ref[:, A_Q_DIM:].astype(_f32)
    v_all = vz_ref[:, :A_KV_DIM].astype(_f32)
    z_all = _lane_chunks(vz_ref, A_KV_DIM, A_Q_DIM, dtype=_f32)
    keep = WINDOW - DEC_SEQ
    problems = []
    for s in range(SAMPLE_SEQS):
        rows = slice(s * DEC_SEQ, (s + 1) * DEC_SEQ)
        k_new, v_new = k_all[rows], v_all[rows]
        ko_ref[s, :keep] = ck_ref[s, DEC_SEQ:]
        ko_ref[s, keep:] = k_new
        vo_ref[s, :keep] = cv_ref[s, DEC_SEQ:]
        vo_ref[s, keep:] = v_new
        for ref, cache, new in ((kk_ref, ck_ref, k_new), (vv_ref, cv_ref, v_new)):
            ref[s, :WINDOW] = cache[s]
            ref[s, WINDOW:WINDOW + DEC_SEQ] = new
            ref[s, WINDOW + DEC_SEQ:] = jnp.zeros((WINDOW - DEC_SEQ, A_KV_DIM), _f32)
        problems.append(([q[rows] for q in q_all], kk_ref[s], vv_ref[s], [z[rows] for z in z_all]))
    g_parts = _attn_core_batch(problems, bias_ref[...], sinks_ref)
    for c in range(A_Q_DIM // LANES):
        g_ref[step_rows, c * LANES:(c + 1) * LANES] = jnp.concatenate([p[c] for p in g_parts], axis=0).astype(g_ref.dtype)

    @pl.when(step == pl.num_programs(0) - 1)
    def _():
        _project_sample_rows(final, g_ref, w_ref, x_ref, gain_ref, res_refs)


def _sample_bias():
    t = jnp.arange(DEC_SEQ)[:, None]
    c = jnp.arange(2 * WINDOW)[None, :]
    ok = (c > t) & (c <= t + WINDOW)
    return jnp.where(ok, 0.0, NEG_BIG).astype(_f32)


def _attn_sample(qk, vz, sinks, cache_k, cache_v, layer, w_out, x, x_block, gain, into=None, kv_out=None):
    srows = SAMPLE_SEQS * DEC_SEQ
    row0 = P_ROWS // srows
    proj_in_specs, res_specs, res_shapes = _sample_projection_specs(layer, x_block, into)
    aliased = ([] if into is None else list(into)) + ([] if kv_out is None else list(kv_out))
    n_in = 9
    aliases = {} if into is None else {n_in + k: k for k in range(len(into))}
    if kv_out is not None:
        aliases.update({n_in + len(aliased) - 2 + k: len(res_specs) + k for k in range(2)})
    kv_spec = pl.BlockSpec((None, SAMPLE_SEQS, WINDOW, A_KV_DIM), lambda b: (layer, b, 0, 0))
    kv_shape = jax.ShapeDtypeStruct((N_A_LAYERS, DEC_BATCH, WINDOW, A_KV_DIM), _f32)
    outs = pl.pallas_call(
        functools.partial(_attn_sample_kernel, into is None, len(aliased)),
        grid=(DEC_BATCH // SAMPLE_SEQS,),
        in_specs=[
            pl.BlockSpec(memory_space=pltpu.SMEM),
            pl.BlockSpec((srows, A_HALF_DIM), lambda b: (row0 + b, 0)),
            pl.BlockSpec((srows, A_HALF_DIM), lambda b: (row0 + b, 0)),
            pl.BlockSpec((DEC_SEQ, 2 * WINDOW), lambda b: (0, 0)),
            kv_spec, kv_spec,
        ] + proj_in_specs + [pl.BlockSpec(memory_space=pl.ANY)] * len(aliased),
        out_specs=res_specs + [kv_spec, kv_spec],
        out_shape=res_shapes + [kv_shape, kv_shape],
        scratch_shapes=[pltpu.VMEM((SAMPLE_SEQS, 2 * WINDOW, A_KV_DIM), _f32)] * 2
                       + [pltpu.VMEM((S_ROWS, A_Q_DIM), _bf16)],
        input_output_aliases=aliases,
        compiler_params=_cparams(1),
        name="attn_sample",
    )(sinks, qk, vz, _sample_bias(), cache_k, cache_v, w_out, x, gain, *aliased)
    return outs[:-2], outs[-2], outs[-1]


CONV_COLS = 256


def _conv_taps(u, u1, u2, w_ref):
    return w_ref[0:1, :] * u2 + w_ref[1:2, :] * u1 + w_ref[2:3, :] * u


CONV_ROW_PARTS = 3
CONV_PART = MM_ROW_TILE // CONV_ROW_PARTS
CONV_TAIL_ROWS = [b * SEQ + SEQ - SUBLANES for b in range(BATCH)]
SAMPLE_ROW0_IN_TILE = P_ROWS - (ROWS // MM_ROW_TILE - 1) * MM_ROW_TILE


def _conv_mixer_kernel(h_ref, w_bg_ref, w_cg_ref, w_x_ref, w_z_ref, cw_ref, p1_ref, p2_ref,
                       g_ref, tail_ref, us_ref, wcat_ref, carry_ref):
    m = pl.program_id(1)

    @pl.when(m == 0)
    def _():
        for k, w_ref in enumerate((w_bg_ref, w_cg_ref, w_x_ref, w_z_ref)):
            wcat_ref[:, k * CONV_COLS:(k + 1) * CONV_COLS] = w_ref[...].astype(_bf16)
        carry_ref[...] = jnp.zeros_like(carry_ref)

    prev2 = carry_ref[SUBLANES - 2:SUBLANES - 1, :]
    prev1 = carry_ref[SUBLANES - 1:SUBLANES, :]
    for r in range(CONV_ROW_PARTS):
        rows = slice(r * CONV_PART, (r + 1) * CONV_PART)
        y = jnp.dot(h_ref[rows, :], wcat_ref[...], preferred_element_type=_f32)
        bg, cg, xi, z = [y[:, k * CONV_COLS:(k + 1) * CONV_COLS] for k in range(4)]
        u = cg * xi
        local = lax.broadcasted_iota(jnp.int32, u.shape, 0)
        row = m * MM_ROW_TILE + r * CONV_PART + local
        is_sample = row >= P_ROWS
        step = jnp.where(is_sample, row % DEC_SEQ, row % SEQ)
        p1 = p2 = 0.0
        lo = max(SAMPLE_ROW0_IN_TILE, r * CONV_PART) - r * CONV_PART
        if lo < CONV_PART:
            head = jnp.zeros((lo, CONV_COLS), _f32)
            past = slice(lo + r * CONV_PART - SAMPLE_ROW0_IN_TILE, None)
            p1 = jnp.where(is_sample, jnp.concatenate([head, p1_ref[past, :]], axis=0), 0.0)
            p2 = jnp.where(is_sample, jnp.concatenate([head, p2_ref[past, :]], axis=0), 0.0)
        u1 = jnp.where(local == 0, prev1, pltpu.roll(u, 1, 0))
        u2 = jnp.where(local == 0, prev2, jnp.where(local == 1, prev1, pltpu.roll(u, 2, 0)))
        u1 = jnp.where(step == 0, p1, u1)
        u2 = jnp.where(step <= 1, p2, u2)
        conv = _conv_taps(u, u1, u2, cw_ref)
        g_ref[rows, :] = (_silu(z) * bg * conv).astype(g_ref.dtype)
        prev2, prev1 = u[CONV_PART - 2:CONV_PART - 1, :], u[CONV_PART - 1:CONV_PART, :]
        for b, tail_row in enumerate(CONV_TAIL_ROWS):
            off = tail_row % MM_ROW_TILE - r * CONV_PART
            if 0 <= off < CONV_PART:
                hit = jnp.broadcast_to(m == tail_row // MM_ROW_TILE, (SUBLANES, CONV_COLS))
                pltpu.store(tail_ref.at[b], u[off:off + SUBLANES], mask=hit)
        lo = max(SAMPLE_ROW0_IN_TILE, r * CONV_PART)
        hi = (r + 1) * CONV_PART
        if lo < hi:
            us_ref[lo - SAMPLE_ROW0_IN_TILE:hi - SAMPLE_ROW0_IN_TILE, :] = u[lo - r * CONV_PART:]
    carry_ref[...] = u[CONV_PART - SUBLANES:]


def _conv_mixer(h, w, wj, conv_w, p1, p2):
    ncb = D_MODEL // CONV_COLS

    def wseg(k):
        return pl.BlockSpec((None, D_MODEL, CONV_COLS), lambda c, m: (wj, 0, k * ncb + c))

    tile_cols = pl.BlockSpec((S_ROWS, CONV_COLS), lambda c, m: (0, c))
    return pl.pallas_call(
        _conv_mixer_kernel,
        grid=(ncb, ROWS // MM_ROW_TILE),
        in_specs=[pl.BlockSpec((MM_ROW_TILE, D_MODEL), lambda c, m: (m, 0)),
                  wseg(0), wseg(1), wseg(2), wseg(3),
                  pl.BlockSpec((CONV_WIDTH, CONV_COLS), lambda c, m: (0, c)), tile_cols, tile_cols],
        out_specs=[pl.BlockSpec((MM_ROW_TILE, CONV_COLS), lambda c, m: (m, c)),
                   pl.BlockSpec((BATCH, SUBLANES, CONV_COLS), lambda c, m: (0, 0, c)),
                   pl.BlockSpec((S_ROWS, CONV_COLS), lambda c, m: (0, c))],
        out_shape=[jax.ShapeDtypeStruct((ROWS, D_MODEL), _bf16),
                   jax.ShapeDtypeStruct((BATCH, SUBLANES, D_MODEL), _f32),
                   jax.ShapeDtypeStruct((S_ROWS, D_MODEL), _f32)],
        scratch_shapes=[pltpu.VMEM((D_MODEL, 4 * CONV_COLS), _bf16), pltpu.VMEM((SUBLANES, CONV_COLS), _f32)],
        compiler_params=_cparams(2),
        name="conv_mixer",
    )(h, w, w, w, w, conv_w, p1, p2)


def _conv_past_rows(state):
    pad_to_seq = lambda a: jnp.pad(a, ((0, 0), (0, DEC_SEQ - a.shape[1]), (0, 0))).reshape(S_ROWS, D_MODEL)
    return pad_to_seq(state[:, 1:2]), pad_to_seq(state)


HG_ROWS = 256
HG_INTER_LAG = 1


def _lower_bound(logit_ref, layer):
    lg = logit_ref[...]
    e = jnp.exp(lg - jnp.max(lg, axis=0, keepdims=True))
    return jnp.sum(e[1:layer + 1], axis=0, keepdims=True) / jnp.sum(e, axis=0, keepdims=True)


def _hgrn_gates(q, fl, lb):
    f = lb + (1.0 - lb) * jax.nn.sigmoid(fl)
    return _silu(q), 1.0 - f, jnp.log2(f)


def _chunk_cumsum(x, chunk):
    t = lax.broadcasted_iota(jnp.int32, x.shape, 0) % chunk
    s = 1
    while s < chunk:
        x = x + jnp.where(t >= s, pltpu.roll(x, s, 0), 0.0)
        s *= 2
    return x


def _hgrn_out(o, z, ng):
    ms = jnp.mean(o * o, axis=-1, keepdims=True)
    return _silu(z) * (o * lax.rsqrt(ms + RMS_EPS) * ng)


def _hgrn_prompt_prep(lb, q, fl, v):
    n_chunks = HG_ROWS // C_CHUNK
    qq, kk, lf = _hgrn_gates(q, fl, lb)
    b = _chunk_cumsum(lf, C_CHUNK)
    b_last = [b[(c + 1) * C_CHUNK - 1:(c + 1) * C_CHUNK] for c in range(n_chunks)]
    b_end = jnp.concatenate([jnp.broadcast_to(r, (C_CHUNK, C_KEY_DIM)) for r in b_last], axis=0)
    q_t = (qq * jnp.exp2(b)).astype(_bf16)
    k_intra = (kk * jnp.exp2(-b)).astype(_bf16)
    k_state = (kk * jnp.exp2(b_end - b)).astype(_bf16)
    return q_t, k_intra, k_state, v.astype(_bf16), b_last


def _hgrn_stage_scores(q_t, k_intra):
    nt = (((1,), (1,)), ((), ()))
    return lax.dot_general(q_t, k_intra, nt, preferred_element_type=_f32)


def _hgrn_stage_intra(scores, k_state, vb):
    tn = (((0,), (0,)), ((), ()))
    row = lax.broadcasted_iota(jnp.int32, (HG_ROWS, HG_ROWS), 0)
    col = lax.broadcasted_iota(jnp.int32, (HG_ROWS, HG_ROWS), 1)
    keep = (col <= row) & (col >= row - row % C_CHUNK)
    att = jnp.where(keep, scores, 0.0)
    o_intra = jnp.dot(att.astype(_bf16), vb, preferred_element_type=_f32)
    sl = [slice(c * C_CHUNK, (c + 1) * C_CHUNK) for c in range(HG_ROWS // C_CHUNK)]
    upd = [lax.dot_general(vb[r], k_state[r], tn, preferred_element_type=_f32) for r in sl]
    return o_intra, upd


def _hgrn_stage_inter(q_t, b_last, upd, st):
    nt = (((1,), (1,)), ((), ()))
    n_chunks = HG_ROWS // C_CHUNK
    sl = [slice(c * C_CHUNK, (c + 1) * C_CHUNK) for c in range(n_chunks)]
    starts = []
    for c in range(n_chunks):
        starts.append(st.astype(_bf16))
        st = st * jnp.exp2(b_last[c]) + upd[c]
    o_inter = jnp.concatenate([lax.dot_general(q_t[sl[c]], starts[c], nt, preferred_element_type=_f32)
                               for c in range(n_chunks)], axis=0)
    return o_inter, st


def _hgrn_prompt_kernel(layer, q_ref, f_ref, i_ref, z_ref, lg_ref, ng_ref, w_ref, x_ref, gain_ref,
                        xn_ref, h_ref, so_ref, st_ref, g_ref, gdone_ref, wb_ref, y_ref):
    s = pl.program_id(0)
    n_seq_tiles = SEQ // HG_ROWS
    tt = s % n_seq_tiles

    @pl.when(s == 0)
    def _():
        wb_ref[...] = w_ref[...].astype(_bf16)
        gdone_ref[...] = jnp.zeros_like(gdone_ref)

    @pl.when(tt == 0)
    def _():
        st_ref[...] = jnp.zeros_like(st_ref)

    def finish():
        xn = x_ref[...] + y_ref[...]
        xn_ref[...] = xn
        h_ref[...] = _rms_scale(xn, gain_ref[...]).astype(h_ref.dtype)

    @pl.when(s == pl.num_programs(0) - 1)
    def _():
        y_ref[...] = jnp.dot(gdone_ref[...], wb_ref[...], preferred_element_type=_f32)
        finish()

    @pl.when(s < pl.num_programs(0) - 1)
    def _():
        lb_all = _lower_bound(lg_ref, layer)
        lanes = lambda hh: slice(hh * LANES, (hh + 1) * LANES)
        heads_per_group = C_HEADS // (D_MODEL // MXU_COLS)
        prepped, scored, intra = {}, {}, {}
        for t in range(C_HEADS + HG_INTER_LAG):
            if t < C_HEADS:
                q, fl, v = [r[:, lanes(t)].astype(_f32) for r in (q_ref, f_ref, i_ref)]
                prepped[t] = _hgrn_prompt_prep(lb_all[:, lanes(t)], q, fl, v)
                scored[t] = _hgrn_stage_scores(prepped[t][0], prepped[t][1])
            h1 = t - 1
            if 0 <= h1 < C_HEADS:
                _, _, k_state, vb, _ = prepped[h1]
                intra[h1] = _hgrn_stage_intra(scored.pop(h1), k_state, vb)
            h2 = t - HG_INTER_LAG
            if 0 <= h2 < C_HEADS:
                q_t, _, _, _, b_last = prepped.pop(h2)
                o_intra, upd = intra.pop(h2)
                o_inter, st = _hgrn_stage_inter(q_t, b_last, upd, st_ref[h2])
                st_ref[h2] = st
                g_ref[:, lanes(h2)] = _hgrn_out(o_inter + o_intra, z_ref[:, lanes(h2)].astype(_f32),
                                                ng_ref[...]).astype(g_ref.dtype)
            if t < C_HEADS and t % heads_per_group == heads_per_group - 1:
                cols = slice(t // heads_per_group * MXU_COLS, (t // heads_per_group + 1) * MXU_COLS)
                y_ref[:, cols] = jnp.dot(gdone_ref[...], wb_ref[:, cols], preferred_element_type=_f32)
        gdone_ref[...] = g_ref[...]
        finish()

        @pl.when(tt == n_seq_tiles - 1)
        def _():
            for hh in range(C_HEADS):
                so_ref[0, hh] = st_ref[hh].T


def _hgrn_prompt(proj, lb_logits, norm_g, layer, w_out, wj, x, gain):
    nt = SEQ // HG_ROWS
    n_tiles = BATCH * nt
    mix_tile = lambda s: jnp.minimum(s, n_tiles - 1)
    proj_tile = lambda width: pl.BlockSpec((HG_ROWS, width), lambda s: (jnp.maximum(s - 1, 0), 0))
    seg = lambda k: pl.BlockSpec((HG_ROWS, D_MODEL), lambda s: (mix_tile(s), k))
    g_tile = pltpu.VMEM((HG_ROWS, D_MODEL), _bf16)
    xn, h, states = pl.pallas_call(
        functools.partial(_hgrn_prompt_kernel, layer),
        grid=(n_tiles + 1,),
        in_specs=[seg(0), seg(1), seg(2), seg(3),
                  pl.BlockSpec((DEPTH, C_F_DIM), lambda s: (0, 0)),
                  pl.BlockSpec((1, C_VAL_DIM), lambda s: (0, 0)),
                  pl.BlockSpec((None, D_MODEL, D_MODEL), lambda s: (wj, 0, 0), pipeline_mode=pl.Buffered(1)),
                  proj_tile(D_MODEL),
                  pl.BlockSpec((1, D_MODEL), lambda s: (0, 0))],
        out_specs=[proj_tile(D_MODEL), proj_tile(D_MODEL),
                   pl.BlockSpec((1, C_HEADS, C_KEY_DIM, C_VAL_DIM), lambda s: (mix_tile(s) // nt, 0, 0, 0))],
        out_shape=_xn_h_shapes() + [jax.ShapeDtypeStruct((BATCH, C_HEADS, C_KEY_DIM, C_VAL_DIM), _f32)],
        scratch_shapes=[pltpu.VMEM((C_HEADS, C_VAL_DIM, C_KEY_DIM), _f32), g_tile, g_tile,
                        pltpu.VMEM((D_MODEL, D_MODEL), _bf16), pltpu.VMEM((HG_ROWS, D_MODEL), _f32)],
        compiler_params=_cparams(1),
        name="hgrn_prompt",
    )(proj, proj, proj, proj, lb_logits, norm_g, w_out, x, gain)
    return (xn, h), states


def _hgrn_sample_kernel(layer, proj_ref, lg_ref, ng_ref, s_ref, w_ref, x_ref, gain_ref, xn_any, h_any,
                        xn_ref, h_ref, so_ref, g_ref):
    del xn_any, h_any
    step = pl.program_id(0)
    srows = SAMPLE_SEQS * DEC_SEQ
    step_rows = pl.ds(pl.multiple_of(step * srows, srows), srows)
    nt = (((1,), (1,)), ((), ()))
    tn = (((0,), (0,)), ((), ()))
    row = lax.broadcasted_iota(jnp.int32, (DEC_SEQ, DEC_SEQ), 0)
    col = lax.broadcasted_iota(jnp.int32, (DEC_SEQ, DEC_SEQ), 1)
    causal = row >= col
    eye = (lax.broadcasted_iota(jnp.int32, (C_KEY_DIM, C_KEY_DIM), 0)
           == lax.broadcasted_iota(jnp.int32, (C_KEY_DIM, C_KEY_DIM), 1))
    for h in range(C_HEADS):
        hs = slice(h * LANES, (h + 1) * LANES)
        lb = _lower_bound(lg_ref.at[:, hs], layer)
        q, fl, v, z = [proj_ref[:, off + h * LANES:off + (h + 1) * LANES].astype(_f32)
                       for off in (0, C_F_DIM, 2 * C_F_DIM, 2 * C_F_DIM + D_MODEL)]
        qq, kk, lf = _hgrn_gates(q, fl, lb)
        b = _chunk_cumsum(lf, DEC_SEQ)
        q_t = qq * jnp.exp2(b)
        k_intra = kk * jnp.exp2(-b)
        outs = []
        for s in range(SAMPLE_SEQS):
            rows = slice(s * DEC_SEQ, (s + 1) * DEC_SEQ)
            b_last = b[(s + 1) * DEC_SEQ - 1:(s + 1) * DEC_SEQ]
            qb = q_t[rows].astype(_bf16)
            k_state = (kk[rows] * jnp.exp2(b_last - b[rows])).astype(_bf16)
            vb = v[rows].astype(_bf16)
            s0 = s_ref[s, h]
            o_inter = jnp.dot(qb, s0.astype(_bf16), preferred_element_type=_f32)
            att = jnp.where(causal, lax.dot_general(qb, k_intra[rows].astype(_bf16), nt,
                                                    preferred_element_type=_f32), 0.0)
            outs.append(o_inter + jnp.dot(att.astype(_bf16), vb, preferred_element_type=_f32))
            decay = jnp.sum(jnp.where(eye, jnp.broadcast_to(jnp.exp2(b_last), eye.shape), 0.0),
                            axis=1, keepdims=True)
            so_ref[s, h] = decay * s0 + lax.dot_general(k_state, vb, tn, preferred_element_type=_f32)
        g_ref[step_rows, hs] = _hgrn_out(jnp.concatenate(outs, axis=0), z, ng_ref[...]).astype(g_ref.dtype)

    @pl.when(step == pl.num_programs(0) - 1)
    def _():
        _project_sample_rows(False, g_ref, w_ref, x_ref, gain_ref, (xn_ref, h_ref))


def _hgrn_sample(proj, lb_logits, norm_g, state, layer, w_out, wj, x, gain, into):
    srows = SAMPLE_SEQS * DEC_SEQ
    row0 = P_ROWS // srows
    proj_in_specs, res_specs, res_shapes = _sample_projection_specs(wj, P_ROWS // S_ROWS, into)
    n_in = 7
    state_spec = pl.BlockSpec((SAMPLE_SEQS, C_HEADS, C_KEY_DIM, C_VAL_DIM), lambda b: (b, 0, 0, 0))
    outs = pl.pallas_call(
        functools.partial(_hgrn_sample_kernel, layer),
        grid=(DEC_BATCH // SAMPLE_SEQS,),
        in_specs=[pl.BlockSpec((srows, C_IN_DIM), lambda b: (row0 + b, 0)),
                  pl.BlockSpec((DEPTH, C_F_DIM), lambda b: (0, 0)),
                  pl.BlockSpec((1, C_VAL_DIM), lambda b: (0, 0)),
                  state_spec] + proj_in_specs + [pl.BlockSpec(memory_space=pl.ANY)] * len(into),
        out_specs=res_specs + [state_spec],
        out_shape=res_shapes + [jax.ShapeDtypeStruct((DEC_BATCH, C_HEADS, C_KEY_DIM, C_VAL_DIM), _f32)],
        scratch_shapes=[pltpu.VMEM((S_ROWS, D_MODEL), _bf16)],
        input_output_aliases={n_in + k: k for k in range(len(into))},
        compiler_params=_cparams(1),
        name="hgrn_sample",
    )(proj, lb_logits, norm_g, state, w_out, x, gain, *into)
    return outs[:-1], outs[-1]


def kernel(x_prompt, x_sample, cache_k, cache_v, state_conv, state_hgrn, ln_g, final_g, a_w_in, a_w_out, a_sinks,
           b_w_in, b_conv_w, b_w_out, c_w_in, c_norm_g, c_w_out, c_lb_logits):
    xp = x_prompt.reshape(P_ROWS, D_MODEL)
    xs = x_sample.reshape(S_ROWS, D_MODEL)
    tabs = jnp.asarray(_rope_tables())
    norm_gains = [ln_g[li].reshape(1, D_MODEL) for li in range(1, DEPTH)] + [final_g.reshape(1, D_MODEL)]

    h = _first_norm(xp, xs, ln_g[0].reshape(1, D_MODEL))
    x = None
    ks_p, vs_p, conv_p, conv_s, hg_p, hg_s = [], [], [], [], [], []
    kv_sample = None
    caches = [c.reshape(N_A_LAYERS, DEC_BATCH, WINDOW, A_KV_DIM) for c in (cache_k, cache_v)]
    for li in range(DEPTH):
        kind, j = li % 3, li // 3
        if kind == 0:
            qk = _attn_in_proj(h, a_w_in, j, 0, rope_tabs=tabs)
            vz = _attn_in_proj(h, a_w_in, j, A_HALF_DIM)
            final = li == DEPTH - 1
            prompt_res, k_p, v_p = _attn_prompt(qk, vz, a_sinks[j], a_w_out, j, xp if li == 0 else x, norm_gains[li],
                                                final)
            ks_p.append(k_p), vs_p.append(v_p)
            x_s, x_block = (xs, 0) if li == 0 else (x, P_ROWS // S_ROWS)
            res, *kv_sample = _attn_sample(qk, vz, a_sinks[j], caches[0], caches[1], j, a_w_out, x_s, x_block,
                                           norm_gains[li], None if final else prompt_res, kv_sample)
            if final:
                (y_p,), (y_s,) = prompt_res, res
            else:
                x, h = res
            continue
        elif kind == 1:
            g, tail, u_s = _conv_mixer(h, b_w_in, j, b_conv_w[j], *_conv_past_rows(state_conv[j]))
            conv_p.append(tail[:, SUBLANES - (CONV_WIDTH - 1):])
            conv_s.append(u_s.reshape(DEC_BATCH, DEC_SEQ, D_MODEL)[:, DEC_SEQ - (CONV_WIDTH - 1):])
            w_out = b_w_out
        else:
            proj = _in_proj(h, c_w_in, j, C_IN_DIM, MM_COL_TILE)
            ng = c_norm_g[j].reshape(1, C_VAL_DIM)
            res, s_p = _hgrn_prompt(proj, c_lb_logits, ng, li, c_w_out, j, x, norm_gains[li])
            (x, h), s_s = _hgrn_sample(proj, c_lb_logits, ng, state_hgrn[j], li, c_w_out, j, x, norm_gains[li], res)
            hg_p.append(s_p), hg_s.append(s_s)
            continue
        x, h = _out_proj(g, w_out, j, x, norm_gains[li])

    def stacked(parts, *shape):
        arr = parts[0][None] if len(parts) == 1 else jnp.stack(parts)
        return arr.reshape(len(parts), *shape)

    kv_shape = (WINDOW, A_KV_HEADS, A_HEAD_DIM)
    return (y_p.reshape(BATCH, SEQ, D_MODEL), y_s.reshape(DEC_BATCH, DEC_SEQ, D_MODEL),
            stacked(ks_p, BATCH, *kv_shape), stacked(vs_p, BATCH, *kv_shape),
            kv_sample[0].reshape(N_A_LAYERS, DEC_BATCH, *kv_shape),
            kv_sample[1].reshape(N_A_LAYERS, DEC_BATCH, *kv_shape),
            stacked(conv_p, BATCH, CONV_WIDTH - 1, D_MODEL), stacked(conv_s, DEC_BATCH, CONV_WIDTH - 1, D_MODEL),
            stacked(hg_p, BATCH, C_HEADS, C_KEY_DIM, C_VAL_DIM),
            stacked(hg_s, DEC_BATCH, C_HEADS, C_KEY_DIM, C_VAL_DIM))
```

```python
import functools

import numpy as np

import jax
import jax.numpy as jnp
from jax import lax
from jax.experimental import pallas as pl
from jax.experimental.pallas import tpu as pltpu

D_MODEL = 2048
BATCH = 4
SEQ = 2048
DEPTH = 4
N_A_LAYERS = (DEPTH + 2) // 3
DEC_BATCH = 32
DEC_SEQ = 8
PAST_LEN = 16384
RMS_EPS = 1e-6

A_HEADS = 32
A_KV_HEADS = 4
A_HEAD_DIM = 64
A_GROUP = A_HEADS // A_KV_HEADS
A_Q_DIM = A_HEADS * A_HEAD_DIM
A_KV_DIM = A_KV_HEADS * A_HEAD_DIM
WINDOW = 128
ROT_DIM = A_HEAD_DIM // 4
ROPE_THETA = 500000.0

CONV_WIDTH = 3

C_HEADS = 16
C_KEY_DIM = 128
C_VAL_DIM = D_MODEL // C_HEADS
C_F_DIM = C_HEADS * C_KEY_DIM
C_IN_DIM = 2 * C_F_DIM + 2 * D_MODEL
C_CHUNK = 32

LANES = 128
SUBLANES = 8
MXU_COLS = 256
P_ROWS = BATCH * SEQ
S_ROWS = DEC_BATCH * DEC_SEQ
ROWS = P_ROWS + S_ROWS
ROW_TILE = 384
SAMPLE_SEQS = 4
MM_ROW_TILE = ROWS // 4
MM_COL_TILE = 1024
MM_ROW_PARTS = 6
NEG_BIG = -1e30
LOG2E = 1.4426950408889634
Q_SCALE = A_HEAD_DIM ** -0.5 * LOG2E
VMEM_LIMIT = 56 * 1024 * 1024

_f32 = jnp.float32
_bf16 = jnp.bfloat16


def _cparams(n_axes):
    return pltpu.CompilerParams(dimension_semantics=("arbitrary",) * n_axes, vmem_limit_bytes=VMEM_LIMIT)


def _silu(x):
    return x * jax.nn.sigmoid(x)


def _rms_scale(x, g):
    ms = jnp.mean(x * x, axis=-1, keepdims=True)
    return x * lax.rsqrt(ms + RMS_EPS) * g


NORM_ROW_TILE = 1024


def _first_norm_kernel(x_ref, g_ref, h_ref):
    h_ref[...] = _rms_scale(x_ref[...], g_ref[...]).astype(h_ref.dtype)


def _first_norm_sample_kernel(x_ref, g_ref, hin_ref, h_ref):
    del hin_ref
    h_ref[...] = _rms_scale(x_ref[...], g_ref[...]).astype(h_ref.dtype)


def _first_norm(xp, xs, g):
    gain_spec = pl.BlockSpec((1, D_MODEL), lambda m: (0, 0))
    h = pl.pallas_call(
        _first_norm_kernel,
        grid=(P_ROWS // NORM_ROW_TILE,),
        in_specs=[pl.BlockSpec((NORM_ROW_TILE, D_MODEL), lambda m: (m, 0)), gain_spec],
        out_specs=pl.BlockSpec((NORM_ROW_TILE, D_MODEL), lambda m: (m, 0)),
        out_shape=jax.ShapeDtypeStruct((ROWS, D_MODEL), _bf16),
        compiler_params=_cparams(1),
        name="first_norm",
    )(xp, g)
    return pl.pallas_call(
        _first_norm_sample_kernel,
        grid=(1,),
        in_specs=[pl.BlockSpec((S_ROWS, D_MODEL), lambda m: (0, 0)), gain_spec, pl.BlockSpec(memory_space=pl.ANY)],
        out_specs=pl.BlockSpec((S_ROWS, D_MODEL), lambda m: (P_ROWS // S_ROWS, 0)),
        out_shape=jax.ShapeDtypeStruct((ROWS, D_MODEL), _bf16),
        input_output_aliases={2: 0},
        compiler_params=_cparams(1),
        name="first_norm_sample",
    )(xs, g, h)


def _in_proj_kernel(h_ref, w_ref, o_ref, wb_ref):
    @pl.when(pl.program_id(1) == 0)
    def _():
        wb_ref[...] = w_ref[...].astype(_bf16)

    part = MM_ROW_TILE // MM_ROW_PARTS
    for r in range(MM_ROW_PARTS):
        rows = slice(r * part, (r + 1) * part)
        o_ref[rows, :] = jnp.dot(h_ref[rows, :], wb_ref[...], preferred_element_type=_f32).astype(o_ref.dtype)


def _in_proj(h, w, wj, n_out, tn):
    return pl.pallas_call(
        _in_proj_kernel,
        grid=(n_out // tn, ROWS // MM_ROW_TILE),
        in_specs=[pl.BlockSpec((MM_ROW_TILE, D_MODEL), lambda j, m: (m, 0)),
                  pl.BlockSpec((None, D_MODEL, tn), lambda j, m: (wj, 0, j))],
        out_specs=pl.BlockSpec((MM_ROW_TILE, tn), lambda j, m: (m, j)),
        out_shape=jax.ShapeDtypeStruct((ROWS, n_out), _bf16),
        scratch_shapes=[pltpu.VMEM((D_MODEL, tn), _bf16)],
        compiler_params=_cparams(2),
        name="in_proj",
    )(h, w)


A_ROW_TILE = ROWS // 8
A_ROW_PARTS = 3


def _attn_in_proj_kernel(rope, h_ref, w_ref, *refs):
    tab_ref = refs[0] if rope else None
    o_ref, wb_ref = refs[-2:]
    s = pl.program_id(0)
    n_col_tiles = A_HALF_DIM // A_COL_TILE
    part = A_ROW_TILE // A_ROW_PARTS
    half = ROT_DIM // 2

    def project(col0, width):
        for r in range(A_ROW_PARTS):
            rows = slice(r * part, (r + 1) * part)
            y = jnp.dot(h_ref[rows, :], wb_ref[:, col0:col0 + width], preferred_element_type=_f32)
            if not rope:
                o_ref[rows, col0:col0 + width] = y.astype(o_ref.dtype)
                continue
            tabs = [tab_ref[k, rows, :] for k in range(3)]
            for c in range(width // LANES):
                col = col0 + c * LANES
                yc = y[:, c * LANES:(c + 1) * LANES]
                rc = yc * tabs[0] + pltpu.roll(yc, LANES - half, 1) * tabs[1] + pltpu.roll(yc, half, 1) * tabs[2]
                if col < A_Q_DIM:
                    rc = rc * Q_SCALE
                o_ref[rows, col:col + LANES] = rc.astype(o_ref.dtype)

    for c in range(n_col_tiles):
        @pl.when(s == c)
        def _():
            wb_ref[:, c * A_COL_TILE:(c + 1) * A_COL_TILE] = w_ref[...].astype(_bf16)
            project(c * A_COL_TILE, A_COL_TILE)

    @pl.when(s >= n_col_tiles)
    def _():
        project(0, A_HALF_DIM)


def _attn_in_proj(h, w, wj, col0, rope_tabs=None):
    n_col_tiles = A_HALF_DIM // A_COL_TILE
    cb0 = col0 // A_COL_TILE
    row_tile = lambda s: jnp.maximum(s - (n_col_tiles - 1), 0)
    in_specs = [pl.BlockSpec((A_ROW_TILE, D_MODEL), lambda s: (row_tile(s), 0)),
                pl.BlockSpec((None, D_MODEL, A_COL_TILE), lambda s: (wj, 0, cb0 + jnp.minimum(s, n_col_tiles - 1)))]
    args = [h, w]
    if rope_tabs is not None:
        in_specs.append(pl.BlockSpec((3, A_ROW_TILE, LANES), lambda s: (0, row_tile(s), 0)))
        args.append(rope_tabs)
    return pl.pallas_call(
        functools.partial(_attn_in_proj_kernel, rope_tabs is not None),
        grid=(ROWS // A_ROW_TILE + n_col_tiles - 1,),
        in_specs=in_specs,
        out_specs=pl.BlockSpec((A_ROW_TILE, A_HALF_DIM), lambda s: (row_tile(s), 0)),
        out_shape=jax.ShapeDtypeStruct((ROWS, A_HALF_DIM), _bf16),
        scratch_shapes=[pltpu.VMEM((D_MODEL, A_HALF_DIM), _bf16)],
        compiler_params=_cparams(1),
        name="attn_in_proj_rope" if rope_tabs is not None else "attn_in_proj",
    )(*args)


X_RING = 3


def _out_proj_kernel(a_ref, w_ref, x_hbm, g_ref, xn_ref, h_ref, wb_ref, xbuf, xsem):
    m = pl.program_id(0)
    n = pl.num_programs(0)

    def x_copy(tile, slot):
        return pltpu.make_async_copy(x_hbm.at[pl.ds(tile * ROW_TILE, ROW_TILE), :], xbuf.at[slot], xsem.at[slot])

    @pl.when(m == 0)
    def _():
        for t in range(X_RING - 1):
            x_copy(t, t).start()
        wb_ref[...] = w_ref[...].astype(_bf16)

    ahead = m + (X_RING - 1)

    @pl.when(ahead < n)
    def _():
        x_copy(ahead, ahead % X_RING).start()

    slot = m % X_RING
    y = jnp.dot(a_ref[...], wb_ref[...], preferred_element_type=_f32)
    x_copy(m, slot).wait()
    xn = xbuf[slot] + y
    xn_ref[...] = xn
    h_ref[...] = _rms_scale(xn, g_ref[...]).astype(h_ref.dtype)


def _xn_h_shapes():
    return [jax.ShapeDtypeStruct((ROWS, D_MODEL), _f32), jax.ShapeDtypeStruct((ROWS, D_MODEL), _bf16)]


def _out_proj(a, w, wj, x, g):
    tile = pl.BlockSpec((ROW_TILE, D_MODEL), lambda m: (m, 0))
    return pl.pallas_call(
        _out_proj_kernel,
        grid=(ROWS // ROW_TILE,),
        in_specs=[tile,
                  pl.BlockSpec((None, D_MODEL, D_MODEL), lambda m: (wj, 0, 0), pipeline_mode=pl.Buffered(1)),
                  pl.BlockSpec(memory_space=pl.ANY), pl.BlockSpec((1, D_MODEL), lambda m: (0, 0))],
        out_specs=[tile, tile],
        out_shape=_xn_h_shapes(),
        scratch_shapes=[pltpu.VMEM((D_MODEL, D_MODEL), _bf16), pltpu.VMEM((X_RING, ROW_TILE, D_MODEL), _f32),
                        pltpu.SemaphoreType.DMA((X_RING,))],
        compiler_params=_cparams(1),
        name="out_proj",
    )(a, w, x, g)


def _rope_tables():
    pos = np.concatenate([np.arange(P_ROWS) % SEQ, PAST_LEN + np.arange(S_ROWS) % DEC_SEQ]).astype(np.float64)
    half = ROT_DIM // 2
    inv = ROPE_THETA ** (-np.arange(half, dtype=np.float64) * 2.0 / ROT_DIM)
    ang = pos[:, None] * inv[None, :]
    cos, sin = np.cos(ang), np.sin(ang)
    rest = A_HEAD_DIM - ROT_DIM
    c_head = np.concatenate([cos, cos, np.ones((ROWS, rest))], axis=1)
    s1_head = np.concatenate([-sin, np.zeros((ROWS, half + rest))], axis=1)
    s2_head = np.concatenate([np.zeros((ROWS, half)), sin, np.zeros((ROWS, rest))], axis=1)
    rep = LANES // A_HEAD_DIM
    return np.stack([np.tile(t, (1, rep)) for t in (c_head, s1_head, s2_head)]).astype(np.float32)


def _lane_halves(xc, head_in_low_half):
    lane = lax.broadcasted_iota(jnp.int32, xc.shape, 1)
    low = lane < A_HEAD_DIM
    sw = pltpu.roll(xc, A_HEAD_DIM, 1)
    if head_in_low_half:
        return jnp.where(low, xc, 0.0), jnp.where(low, 0.0, sw)
    return jnp.where(low, sw, 0.0), jnp.where(low, 0.0, xc)


def _attn_core_batch(problems, bias, sinks_ref):
    nt = (((1,), (1,)), ((), ()))
    cols_per_kv = A_GROUP * A_HEAD_DIM // LANES
    tq = problems[0][0][0].shape[0]
    lane = lax.broadcasted_iota(jnp.int32, (tq, LANES), 1)
    low = lane < A_HEAD_DIM
    pairs = [(n, j) for n in range(len(problems)) for j in range(A_KV_HEADS)]
    scores, values = {}, {}
    for n, j in pairs:
        q_cols, kk, vv, _ = problems[n]
        kc = kk[:, (j // 2) * LANES:(j // 2 + 1) * LANES]
        vc = vv[:, (j // 2) * LANES:(j // 2 + 1) * LANES]
        k_lo, k_hi = _lane_halves(kc, j % 2 == 0)
        values[n, j] = _lane_halves(vc, j % 2 == 0)
        qs = jnp.concatenate(q_cols[j * cols_per_kv:(j + 1) * cols_per_kv], axis=0).astype(_bf16)
        scores[n, j] = (lax.dot_general(qs, k_lo.astype(_bf16), nt, preferred_element_type=_f32),
                        lax.dot_general(qs, k_hi.astype(_bf16), nt, preferred_element_type=_f32))
    probs = {}
    for n, j in pairs:
        p_lo, p_hi, r_lo, r_hi = [], [], [], []
        for c in range(cols_per_kv):
            for s_all, ps, rs, odd in ((scores[n, j][0], p_lo, r_lo, 0), (scores[n, j][1], p_hi, r_hi, 1)):
                sink = sinks_ref[j * A_GROUP + 2 * c + odd] * LOG2E
                s = s_all[c * tq:(c + 1) * tq] + bias
                m = jnp.maximum(jnp.max(s, axis=-1, keepdims=True), sink)
                p = jnp.exp2(s - m)
                den = jnp.sum(p, axis=-1, keepdims=True) + jnp.exp2(sink - m)
                ps.append(p)
                rs.append(1.0 / den)
        probs[n, j] = (jnp.concatenate(p_lo, axis=0).astype(_bf16), jnp.concatenate(p_hi, axis=0).astype(_bf16),
                       r_lo, r_hi)
    outs = [[] for _ in problems]
    for n, j in pairs:
        p_lo, p_hi, r_lo, r_hi = probs[n, j]
        v_lo, v_hi = values[n, j]
        o = (jnp.dot(p_lo, v_lo.astype(_bf16), preferred_element_type=_f32)
             + jnp.dot(p_hi, v_hi.astype(_bf16), preferred_element_type=_f32))
        z_cols = problems[n][3]
        for c in range(cols_per_kv):
            oc = o[c * tq:(c + 1) * tq] * jnp.where(low, r_lo[c], r_hi[c])
            outs[n].append(_silu(z_cols[j * cols_per_kv + c]) * oc)
    return outs


Q_COLS_PER_KV = A_GROUP * A_HEAD_DIM // LANES


def _attn_stage_scores(q_cols, kk, j):
    nt = (((1,), (1,)), ((), ()))
    k_lo, k_hi = _lane_halves(kk[:, (j // 2) * LANES:(j // 2 + 1) * LANES], j % 2 == 0)
    qs = jnp.concatenate(q_cols, axis=0).astype(_bf16)
    return (lax.dot_general(k_lo.astype(_bf16), qs, nt, preferred_element_type=_f32),
            lax.dot_general(k_hi.astype(_bf16), qs, nt, preferred_element_type=_f32))


def _attn_stage_values(scores, vv, j, bias_t, sinks_ref):
    tq = WINDOW
    zeros = jnp.zeros((A_HEAD_DIM, 2 * WINDOW), _f32)
    srow = lax.broadcasted_iota(jnp.int32, (2 * SUBLANES, 4 * WINDOW), 0)
    scol = lax.broadcasted_iota(jnp.int32, (2 * SUBLANES, 4 * WINDOW), 1)
    sum_rows = jnp.where(srow == scol // (2 * WINDOW), 1.0, 0.0)
    p_lo, p_hi, r_lo, r_hi = [], [], [], []
    for c in range(Q_COLS_PER_KV):
        for st_all, ps, rs, odd in ((scores[0], p_lo, r_lo, 0), (scores[1], p_hi, r_hi, 1)):
            sink = sinks_ref[j * A_GROUP + 2 * c + odd] * LOG2E
            s = st_all[:, c * tq:(c + 1) * tq] + bias_t
            m = jnp.maximum(jnp.max(s, axis=0, keepdims=True), sink)
            ps.append(jnp.exp2(s - m).astype(_bf16))
            rs.append(jnp.exp2(sink - m))
    pt = jnp.concatenate([jnp.concatenate(p_lo, axis=1), jnp.concatenate(p_hi, axis=1)], axis=0)
    vt = vv[:, (j // 2) * LANES:(j // 2 + 1) * LANES].T
    vj = vt[:A_HEAD_DIM] if j % 2 == 0 else vt[A_HEAD_DIM:]
    vcat = jnp.concatenate([jnp.concatenate([vj, zeros], axis=1), jnp.concatenate([zeros, vj], axis=1),
                            sum_rows], axis=0)
    return jnp.dot(vcat.astype(_bf16), pt, preferred_element_type=_f32), (r_lo, r_hi)


def _attn_stage_output(ot, sink_terms, z_cols):
    tq = WINDOW
    low = lax.broadcasted_iota(jnp.int32, (LANES, tq), 0) < A_HEAD_DIM
    out = []
    for c in range(Q_COLS_PER_KV):
        cols = slice(c * tq, (c + 1) * tq)
        den_lo = ot[LANES:LANES + 1, cols] + sink_terms[0][c]
        den_hi = ot[LANES + 1:LANES + 2, cols] + sink_terms[1][c]
        oc = ot[:LANES, cols] * jnp.where(low, 1.0 / den_lo, 1.0 / den_hi)
        out.append(_silu(z_cols[c]) * oc.T)
    return out


def _lane_chunks(ref, start, width, rows=slice(None), dtype=None):
    chunks = [ref[rows, start + c * LANES:start + (c + 1) * LANES] for c in range(width // LANES)]
    return chunks if dtype is None else [c.astype(dtype) for c in chunks]


def _attn_prompt_kernel(final, sinks_ref, qk_ref, vz_ref, bias_ref, w_ref, x_ref, gain_ref, *refs):
    out_refs, (ko_ref, vo_ref, kprev, vprev, g_ref, gdone_ref, wb_ref, y_ref) = refs[:-8], refs[-8:]
    s = pl.program_id(0)
    i = s % (SEQ // (ATTN_BLOCKS_PER_STEP * WINDOW))

    @pl.when(s == 0)
    def _():
        wb_ref[...] = w_ref[...].astype(_bf16)
        gdone_ref[...] = jnp.zeros_like(gdone_ref)

    @pl.when(i == 0)
    def _():
        kprev[...] = jnp.zeros_like(kprev)
        vprev[...] = jnp.zeros_like(vprev)

    def finish():
        xn = x_ref[...] + y_ref[...]
        if final:
            out_refs[0][...] = _rms_scale(xn, gain_ref[...])
        else:
            out_refs[0][...] = xn
            out_refs[1][...] = _rms_scale(xn, gain_ref[...]).astype(out_refs[1].dtype)

    @pl.when(s == pl.num_programs(0) - 1)
    def _():
        y_ref[...] = jnp.dot(gdone_ref[...], wb_ref[...], preferred_element_type=_f32)
        finish()

    @pl.when(s < pl.num_programs(0) - 1)
    def _():
        blocks = []
        k_prev, v_prev = kprev[...], vprev[...]
        for sub in range(ATTN_BLOCKS_PER_STEP):
            rows = slice(sub * WINDOW, (sub + 1) * WINDOW)
            k_new = qk_ref[rows, A_Q_DIM:].astype(_f32)
            v_new = vz_ref[rows, :A_KV_DIM].astype(_f32)
            bias = jnp.where(i == 0, bias_ref[0], bias_ref[1]) if sub == 0 else bias_ref[1]
            blocks.append((rows, jnp.concatenate([k_prev, k_new], axis=0), jnp.concatenate([v_prev, v_new], axis=0), bias))
            k_prev, v_prev = k_new, v_new
        ko_ref[0] = k_prev
        vo_ref[0] = v_prev
        kprev[...] = k_prev
        vprev[...] = v_prev

        items = [(sub, j) for sub in range(ATTN_BLOCKS_PER_STEP) for j in range(A_KV_HEADS)]
        assert len(items) == D_MODEL // MXU_COLS
        scores, values = {}, {}
        for t in range(len(items) + 2):
            if t < len(items):
                sub, j = items[t]
                rows, kk, _, _ = blocks[sub]
                q_cols = _lane_chunks(qk_ref, j * Q_COLS_PER_KV * LANES, Q_COLS_PER_KV * LANES, rows)
                scores[t] = _attn_stage_scores(q_cols, kk, j)
                cols = slice(t * MXU_COLS, (t + 1) * MXU_COLS)
                y_ref[:, cols] = jnp.dot(gdone_ref[...], wb_ref[:, cols], preferred_element_type=_f32)
            if 0 <= t - 1 < len(items):
                sub, j = items[t - 1]
                _, _, vv, bias = blocks[sub]
                values[t - 1] = _attn_stage_values(scores.pop(t - 1), vv, j, bias, sinks_ref)
            if 0 <= t - 2 < len(items):
                sub, j = items[t - 2]
                rows = blocks[sub][0]
                first = j * Q_COLS_PER_KV * LANES
                z_cols = _lane_chunks(vz_ref, A_KV_DIM + first, Q_COLS_PER_KV * LANES, rows, dtype=_f32)
                for c, gc in enumerate(_attn_stage_output(*values.pop(t - 2), z_cols)):
                    g_ref[rows, first + c * LANES:first + (c + 1) * LANES] = gc.astype(g_ref.dtype)
        gdone_ref[...] = g_ref[...]
        finish()


def _prompt_bias_t():
    c = jnp.arange(2 * WINDOW)[:, None]
    r = jnp.arange(WINDOW)[None, :]
    band = (c > r) & (c <= r + WINDOW)
    first = band & (c >= WINDOW)
    return jnp.where(jnp.stack([first, band]), 0.0, NEG_BIG).astype(_f32)


A_HALF_DIM = A_Q_DIM + A_KV_DIM
A_COL_TILE = A_HALF_DIM // 3


ATTN_BLOCKS_PER_STEP = 2


def _attn_prompt(qk, vz, sinks, w_out, wj, x, gain, final):
    rows = ATTN_BLOCKS_PER_STEP * WINDOW
    nb = SEQ // rows
    n_tiles = BATCH * nb
    attn_tile = lambda width: pl.BlockSpec((rows, width), lambda s: (jnp.minimum(s, n_tiles - 1), 0))
    proj_tile = lambda width: pl.BlockSpec((rows, width), lambda s: (jnp.maximum(s - 1, 0), 0))
    kv_spec = pl.BlockSpec((1, WINDOW, A_KV_DIM), lambda s: (jnp.minimum(s, n_tiles - 1) // nb, 0, 0))
    kv_shape = jax.ShapeDtypeStruct((BATCH, WINDOW, A_KV_DIM), _f32)
    if final:
        res_specs, res_shapes = [proj_tile(D_MODEL)], [jax.ShapeDtypeStruct((P_ROWS, D_MODEL), _f32)]
    else:
        res_specs, res_shapes = [proj_tile(D_MODEL), proj_tile(D_MODEL)], _xn_h_shapes()
    g_tile = pltpu.VMEM((rows, A_Q_DIM), _bf16)
    outs = pl.pallas_call(
        functools.partial(_attn_prompt_kernel, final),
        grid=(n_tiles + 1,),
        in_specs=[
            pl.BlockSpec(memory_space=pltpu.SMEM),
            attn_tile(A_HALF_DIM), attn_tile(A_HALF_DIM),
            pl.BlockSpec((2, 2 * WINDOW, WINDOW), lambda s: (0, 0, 0)),
            pl.BlockSpec((None, D_MODEL, D_MODEL), lambda s: (wj, 0, 0), pipeline_mode=pl.Buffered(1)),
            proj_tile(D_MODEL),
            pl.BlockSpec((1, D_MODEL), lambda s: (0, 0)),
        ],
        out_specs=res_specs + [kv_spec, kv_spec],
        out_shape=res_shapes + [kv_shape, kv_shape],
        scratch_shapes=[pltpu.VMEM((WINDOW, A_KV_DIM), _f32), pltpu.VMEM((WINDOW, A_KV_DIM), _f32),
                        g_tile, g_tile, pltpu.VMEM((D_MODEL, D_MODEL), _bf16), pltpu.VMEM((rows, D_MODEL), _f32)],
        compiler_params=_cparams(1),
        name="attn_prompt",
    )(sinks, qk, vz, _prompt_bias_t(), w_out, x, gain)
    return outs[:-2], outs[-2], outs[-1]


SAMPLE_STEPS = DEC_BATCH // SAMPLE_SEQS
SAMPLE_W_ROWS = D_MODEL // SAMPLE_STEPS


def _round_sample_weight_rows(step, w_ref, wb_ref):
    wb_ref[pl.ds(pl.multiple_of(step * SAMPLE_W_ROWS, SAMPLE_W_ROWS), SAMPLE_W_ROWS), :] = w_ref[...].astype(_bf16)


def _project_sample_rows(final, g_ref, wb_ref, x_ref, gain_ref, out_refs):
    g = g_ref[...]
    y = [jnp.dot(g, wb_ref[:, c * MXU_COLS:(c + 1) * MXU_COLS], preferred_element_type=_f32)
         for c in range(D_MODEL // MXU_COLS)]
    xn = x_ref[...] + jnp.concatenate(y, axis=1)
    if final:
        out_refs[0][...] = _rms_scale(xn, gain_ref[...])
    else:
        out_refs[0][...] = xn
        out_refs[1][...] = _rms_scale(xn, gain_ref[...]).astype(out_refs[1].dtype)


def _sample_projection_specs(wj, x_block, into):
    once = lambda shape, index: pl.BlockSpec(shape, index, pipeline_mode=pl.Buffered(1))
    in_specs = [pl.BlockSpec((None, SAMPLE_W_ROWS, D_MODEL), lambda b: (wj, b, 0)),
                once((S_ROWS, D_MODEL), lambda b: (x_block, 0)),
                pl.BlockSpec((1, D_MODEL), lambda b: (0, 0))]
    if into is None:
        return in_specs, [pl.BlockSpec((S_ROWS, D_MODEL), lambda b: (0, 0))], [jax.ShapeDtypeStruct((S_ROWS, D_MODEL), _f32)]
    sample_block = pl.BlockSpec((S_ROWS, D_MODEL), lambda b: (P_ROWS // S_ROWS, 0))
    return in_specs, [sample_block, sample_block], _xn_h_shapes()


def _attn_sample_kernel(final, n_aliased, sinks_ref, qk_ref, vz_ref, bias_ref, ck_ref, cv_ref, w_ref, x_ref, gain_ref,
                        *refs):
    *res_refs, ko_ref, vo_ref, kk_ref, vv_ref, g_ref, wb_ref = refs[n_aliased:]
    step = pl.program_id(0)
    srows = SAMPLE_SEQS * DEC_SEQ
    step_rows = pl.ds(pl.multiple_of(step * srows, srows), srows)
    _round_sample_weight_rows(step, w_ref, wb_ref)
    q_all = _lane_chunks(qk_ref, 0, A_Q_DIM, dtype=_f32)
    k_all = qk_ref[:, A_Q_DIM:].astype(_f32)
    v_all = vz_ref[:, :A_KV_DIM].astype(_f32)
    z_all = _lane_chunks(vz_ref, A_KV_DIM, A_Q_DIM, dtype=_f32)
    keep = WINDOW - DEC_SEQ
    problems = []
    for s in range(SAMPLE_SEQS):
        rows = slice(s * DEC_SEQ, (s + 1) * DEC_SEQ)
        k_new, v_new = k_all[rows], v_all[rows]
        ko_ref[s, :keep] = ck_ref[s, DEC_SEQ:]
        ko_ref[s, keep:] = k_new
        vo_ref[s, :keep] = cv_ref[s, DEC_SEQ:]
        vo_ref[s, keep:] = v_new
        for ref, cache, new in ((kk_ref, ck_ref, k_new), (vv_ref, cv_ref, v_new)):
            ref[s, :WINDOW] = cache[s]
            ref[s, WINDOW:WINDOW + DEC_SEQ] = new
            ref[s, WINDOW + DEC_SEQ:] = jnp.zeros((WINDOW - DEC_SEQ, A_KV_DIM), _f32)
        problems.append(([q[rows] for q in q_all], kk_ref[s], vv_ref[s], [z[rows] for z in z_all]))
    g_parts = _attn_core_batch(problems, bias_ref[...], sinks_ref)
    for c in range(A_Q_DIM // LANES):
        g_ref[step_rows, c * LANES:(c + 1) * LANES] = jnp.concatenate([p[c] for p in g_parts], axis=0).astype(g_ref.dtype)

    @pl.when(step == pl.num_programs(0) - 1)
    def _():
        _project_sample_rows(final, g_ref, wb_ref, x_ref, gain_ref, res_refs)


def _sample_bias():
    t = jnp.arange(DEC_SEQ)[:, None]
    c = jnp.arange(2 * WINDOW)[None, :]
    ok = (c > t) & (c <= t + WINDOW)
    return jnp.where(ok, 0.0, NEG_BIG).astype(_f32)


def _attn_sample(qk, vz, sinks, cache_k, cache_v, layer, w_out, x, x_block, gain, into=None, kv_out=None):
    srows = SAMPLE_SEQS * DEC_SEQ
    row0 = P_ROWS // srows
    proj_in_specs, res_specs, res_shapes = _sample_projection_specs(layer, x_block, into)
    aliased = ([] if into is None else list(into)) + ([] if kv_out is None else list(kv_out))
    n_in = 9
    aliases = {} if into is None else {n_in + k: k for k in range(len(into))}
    if kv_out is not None:
        aliases.update({n_in + len(aliased) - 2 + k: len(res_specs) + k for k in range(2)})
    kv_spec = pl.BlockSpec((None, SAMPLE_SEQS, WINDOW, A_KV_DIM), lambda b: (layer, b, 0, 0))
    kv_shape = jax.ShapeDtypeStruct((N_A_LAYERS, DEC_BATCH, WINDOW, A_KV_DIM), _f32)
    outs = pl.pallas_call(
        functools.partial(_attn_sample_kernel, into is None, len(aliased)),
        grid=(DEC_BATCH // SAMPLE_SEQS,),
        in_specs=[
            pl.BlockSpec(memory_space=pltpu.SMEM),
            pl.BlockSpec((srows, A_HALF_DIM), lambda b: (row0 + b, 0)),
            pl.BlockSpec((srows, A_HALF_DIM), lambda b: (row0 + b, 0)),
            pl.BlockSpec((DEC_SEQ, 2 * WINDOW), lambda b: (0, 0)),
            kv_spec, kv_spec,
        ] + proj_in_specs + [pl.BlockSpec(memory_space=pl.ANY)] * len(aliased),
        out_specs=res_specs + [kv_spec, kv_spec],
        out_shape=res_shapes + [kv_shape, kv_shape],
        scratch_shapes=[pltpu.VMEM((SAMPLE_SEQS, 2 * WINDOW, A_KV_DIM), _f32)] * 2
                       + [pltpu.VMEM((S_ROWS, A_Q_DIM), _bf16), pltpu.VMEM((D_MODEL, D_MODEL), _bf16)],
        input_output_aliases=aliases,
        compiler_params=_cparams(1),
        name="attn_sample",
    )(sinks, qk, vz, _sample_bias(), cache_k, cache_v, w_out, x, gain, *aliased)
    return outs[:-2], outs[-2], outs[-1]


CONV_COLS = 256


def _conv_taps(u, u1, u2, w_ref):
    return w_ref[0:1, :] * u2 + w_ref[1:2, :] * u1 + w_ref[2:3, :] * u


CONV_ROW_PARTS = 3
CONV_PART = MM_ROW_TILE // CONV_ROW_PARTS
CONV_TAIL_ROWS = [b * SEQ + SEQ - SUBLANES for b in range(BATCH)]
SAMPLE_ROW0_IN_TILE = P_ROWS - (ROWS // MM_ROW_TILE - 1) * MM_ROW_TILE


def _conv_mixer_kernel(h_ref, w_bg_ref, w_cg_ref, w_x_ref, w_z_ref, cw_ref, p1_ref, p2_ref,
                       g_ref, tail_ref, us_ref, wcat_ref, carry_ref):
    m = pl.program_id(1)

    @pl.when(m == 0)
    def _():
        for k, w_ref in enumerate((w_bg_ref, w_cg_ref, w_x_ref, w_z_ref)):
            wcat_ref[:, k * CONV_COLS:(k + 1) * CONV_COLS] = w_ref[...].astype(_bf16)
        carry_ref[...] = jnp.zeros_like(carry_ref)

    prev2 = carry_ref[SUBLANES - 2:SUBLANES - 1, :]
    prev1 = carry_ref[SUBLANES - 1:SUBLANES, :]
    for r in range(CONV_ROW_PARTS):
        rows = slice(r * CONV_PART, (r + 1) * CONV_PART)
        y = jnp.dot(h_ref[rows, :], wcat_ref[...], preferred_element_type=_f32)
        bg, cg, xi, z = [y[:, k * CONV_COLS:(k + 1) * CONV_COLS] for k in range(4)]
        u = cg * xi
        local = lax.broadcasted_iota(jnp.int32, u.shape, 0)
        row = m * MM_ROW_TILE + r * CONV_PART + local
        is_sample = row >= P_ROWS
        step = jnp.where(is_sample, row % DEC_SEQ, row % SEQ)
        p1 = p2 = 0.0
        lo = max(SAMPLE_ROW0_IN_TILE, r * CONV_PART) - r * CONV_PART
        if lo < CONV_PART:
            head = jnp.zeros((lo, CONV_COLS), _f32)
            past = slice(lo + r * CONV_PART - SAMPLE_ROW0_IN_TILE, None)
            p1 = jnp.where(is_sample, jnp.concatenate([head, p1_ref[past, :]], axis=0), 0.0)
            p2 = jnp.where(is_sample, jnp.concatenate([head, p2_ref[past, :]], axis=0), 0.0)
        u1 = jnp.where(local == 0, prev1, pltpu.roll(u, 1, 0))
        u2 = jnp.where(local == 0, prev2, jnp.where(local == 1, prev1, pltpu.roll(u, 2, 0)))
        u1 = jnp.where(step == 0, p1, u1)
        u2 = jnp.where(step <= 1, p2, u2)
        conv = _conv_taps(u, u1, u2, cw_ref)
        g_ref[rows, :] = (_silu(z) * bg * conv).astype(g_ref.dtype)
        prev2, prev1 = u[CONV_PART - 2:CONV_PART - 1, :], u[CONV_PART - 1:CONV_PART, :]
        for b, tail_row in enumerate(CONV_TAIL_ROWS):
            off = tail_row % MM_ROW_TILE - r * CONV_PART
            if 0 <= off < CONV_PART:
                hit = jnp.broadcast_to(m == tail_row // MM_ROW_TILE, (SUBLANES, CONV_COLS))
                pltpu.store(tail_ref.at[b], u[off:off + SUBLANES], mask=hit)
        lo = max(SAMPLE_ROW0_IN_TILE, r * CONV_PART)
        hi = (r + 1) * CONV_PART
        if lo < hi:
            us_ref[lo - SAMPLE_ROW0_IN_TILE:hi - SAMPLE_ROW0_IN_TILE, :] = u[lo - r * CONV_PART:]
    carry_ref[...] = u[CONV_PART - SUBLANES:]


def _conv_mixer(h, w, wj, conv_w, p1, p2):
    ncb = D_MODEL // CONV_COLS

    def wseg(k):
        return pl.BlockSpec((None, D_MODEL, CONV_COLS), lambda c, m: (wj, 0, k * ncb + c))

    tile_cols = pl.BlockSpec((S_ROWS, CONV_COLS), lambda c, m: (0, c))
    return pl.pallas_call(
        _conv_mixer_kernel,
        grid=(ncb, ROWS // MM_ROW_TILE),
        in_specs=[pl.BlockSpec((MM_ROW_TILE, D_MODEL), lambda c, m: (m, 0)),
                  wseg(0), wseg(1), wseg(2), wseg(3),
                  pl.BlockSpec((CONV_WIDTH, CONV_COLS), lambda c, m: (0, c)), tile_cols, tile_cols],
        out_specs=[pl.BlockSpec((MM_ROW_TILE, CONV_COLS), lambda c, m: (m, c)),
                   pl.BlockSpec((BATCH, SUBLANES, CONV_COLS), lambda c, m: (0, 0, c)),
                   pl.BlockSpec((S_ROWS, CONV_COLS), lambda c, m: (0, c))],
        out_shape=[jax.ShapeDtypeStruct((ROWS, D_MODEL), _bf16),
                   jax.ShapeDtypeStruct((BATCH, SUBLANES, D_MODEL), _f32),
                   jax.ShapeDtypeStruct((S_ROWS, D_MODEL), _f32)],
        scratch_shapes=[pltpu.VMEM((D_MODEL, 4 * CONV_COLS), _bf16), pltpu.VMEM((SUBLANES, CONV_COLS), _f32)],
        compiler_params=_cparams(2),
        name="conv_mixer",
    )(h, w, w, w, w, conv_w, p1, p2)


def _conv_past_rows(state):
    pad_to_seq = lambda a: jnp.pad(a, ((0, 0), (0, DEC_SEQ - a.shape[1]), (0, 0))).reshape(S_ROWS, D_MODEL)
    return pad_to_seq(state[:, 1:2]), pad_to_seq(state)


HG_ROWS = 256
HG_INTER_LAG = 1


def _lower_bound(logit_ref, layer):
    lg = logit_ref[...]
    e = jnp.exp(lg - jnp.max(lg, axis=0, keepdims=True))
    return jnp.sum(e[1:layer + 1], axis=0, keepdims=True) / jnp.sum(e, axis=0, keepdims=True)


def _hgrn_gates(q, fl, lb):
    f = lb + (1.0 - lb) * jax.nn.sigmoid(fl)
    return _silu(q), 1.0 - f, jnp.log2(f)


def _chunk_cumsum(x, chunk):
    t = lax.broadcasted_iota(jnp.int32, x.shape, 0) % chunk
    s = 1
    while s < chunk:
        x = x + jnp.where(t >= s, pltpu.roll(x, s, 0), 0.0)
        s *= 2
    return x


def _hgrn_out(o, z, ng):
    ms = jnp.mean(o * o, axis=-1, keepdims=True)
    return _silu(z) * (o * lax.rsqrt(ms + RMS_EPS) * ng)


def _hgrn_prompt_prep(lb, q, fl, v):
    n_chunks = HG_ROWS // C_CHUNK
    qq, kk, lf = _hgrn_gates(q, fl, lb)
    b = _chunk_cumsum(lf, C_CHUNK)
    b_last = [b[(c + 1) * C_CHUNK - 1:(c + 1) * C_CHUNK] for c in range(n_chunks)]
    b_end = jnp.concatenate([jnp.broadcast_to(r, (C_CHUNK, C_KEY_DIM)) for r in b_last], axis=0)
    q_t = (qq * jnp.exp2(b)).astype(_bf16)
    k_intra = (kk * jnp.exp2(-b)).astype(_bf16)
    k_state = (kk * jnp.exp2(b_end - b)).astype(_bf16)
    return q_t, k_intra, k_state, v.astype(_bf16), b_last


def _hgrn_stage_scores(q_t, k_intra):
    nt = (((1,), (1,)), ((), ()))
    return lax.dot_general(q_t, k_intra, nt, preferred_element_type=_f32)


def _hgrn_stage_intra(scores, k_state, vb):
    tn = (((0,), (0,)), ((), ()))
    row = lax.broadcasted_iota(jnp.int32, (HG_ROWS, HG_ROWS), 0)
    col = lax.broadcasted_iota(jnp.int32, (HG_ROWS, HG_ROWS), 1)
    keep = (col <= row) & (col >= row - row % C_CHUNK)
    att = jnp.where(keep, scores, 0.0)
    o_intra = jnp.dot(att.astype(_bf16), vb, preferred_element_type=_f32)
    sl = [slice(c * C_CHUNK, (c + 1) * C_CHUNK) for c in range(HG_ROWS // C_CHUNK)]
    upd = [lax.dot_general(vb[r], k_state[r], tn, preferred_element_type=_f32) for r in sl]
    return o_intra, upd


def _hgrn_stage_inter(q_t, b_last, upd, st):
    nt = (((1,), (1,)), ((), ()))
    n_chunks = HG_ROWS // C_CHUNK
    sl = [slice(c * C_CHUNK, (c + 1) * C_CHUNK) for c in range(n_chunks)]
    starts = []
    for c in range(n_chunks):
        starts.append(st.astype(_bf16))
        st = st * jnp.exp2(b_last[c]) + upd[c]
    o_inter = jnp.concatenate([lax.dot_general(q_t[sl[c]], starts[c], nt, preferred_element_type=_f32)
                               for c in range(n_chunks)], axis=0)
    return o_inter, st


def _hgrn_prompt_kernel(layer, q_ref, f_ref, i_ref, z_ref, lg_ref, ng_ref, w_ref, x_ref, gain_ref,
                        xn_ref, h_ref, so_ref, st_ref, g_ref, gdone_ref, wb_ref, y_ref):
    s = pl.program_id(0)
    n_seq_tiles = SEQ // HG_ROWS
    tt = s % n_seq_tiles

    @pl.when(s == 0)
    def _():
        wb_ref[...] = w_ref[...].astype(_bf16)
        gdone_ref[...] = jnp.zeros_like(gdone_ref)

    @pl.when(tt == 0)
    def _():
        st_ref[...] = jnp.zeros_like(st_ref)

    def finish():
        xn = x_ref[...] + y_ref[...]
        xn_ref[...] = xn
        h_ref[...] = _rms_scale(xn, gain_ref[...]).astype(h_ref.dtype)

    @pl.when(s == pl.num_programs(0) - 1)
    def _():
        y_ref[...] = jnp.dot(gdone_ref[...], wb_ref[...], preferred_element_type=_f32)
        finish()

    @pl.when(s < pl.num_programs(0) - 1)
    def _():
        lb_all = _lower_bound(lg_ref, layer)
        lanes = lambda hh: slice(hh * LANES, (hh + 1) * LANES)
        heads_per_group = C_HEADS // (D_MODEL // MXU_COLS)
        prepped, scored, intra = {}, {}, {}
        for t in range(C_HEADS + HG_INTER_LAG):
            if t < C_HEADS:
                q, fl, v = [r[:, lanes(t)].astype(_f32) for r in (q_ref, f_ref, i_ref)]
                prepped[t] = _hgrn_prompt_prep(lb_all[:, lanes(t)], q, fl, v)
                scored[t] = _hgrn_stage_scores(prepped[t][0], prepped[t][1])
            h1 = t - 1
            if 0 <= h1 < C_HEADS:
                _, _, k_state, vb, _ = prepped[h1]
                intra[h1] = _hgrn_stage_intra(scored.pop(h1), k_state, vb)
            h2 = t - HG_INTER_LAG
            if 0 <= h2 < C_HEADS:
                q_t, _, _, _, b_last = prepped.pop(h2)
                o_intra, upd = intra.pop(h2)
                o_inter, st = _hgrn_stage_inter(q_t, b_last, upd, st_ref[h2])
                st_ref[h2] = st
                g_ref[:, lanes(h2)] = _hgrn_out(o_inter + o_intra, z_ref[:, lanes(h2)].astype(_f32),
                                                ng_ref[...]).astype(g_ref.dtype)
            if t < C_HEADS and t % heads_per_group == heads_per_group - 1:
                cols = slice(t // heads_per_group * MXU_COLS, (t // heads_per_group + 1) * MXU_COLS)
                y_ref[:, cols] = jnp.dot(gdone_ref[...], wb_ref[:, cols], preferred_element_type=_f32)
        gdone_ref[...] = g_ref[...]
        finish()

        @pl.when(tt == n_seq_tiles - 1)
        def _():
            for hh in range(C_HEADS):
                so_ref[0, hh] = st_ref[hh].T


def _hgrn_prompt(proj, lb_logits, norm_g, layer, w_out, wj, x, gain):
    nt = SEQ // HG_ROWS
    n_tiles = BATCH * nt
    mix_tile = lambda s: jnp.minimum(s, n_tiles - 1)
    proj_tile = lambda width: pl.BlockSpec((HG_ROWS, width), lambda s: (jnp.maximum(s - 1, 0), 0))
    seg = lambda k: pl.BlockSpec((HG_ROWS, D_MODEL), lambda s: (mix_tile(s), k))
    g_tile = pltpu.VMEM((HG_ROWS, D_MODEL), _bf16)
    xn, h, states = pl.pallas_call(
        functools.partial(_hgrn_prompt_kernel, layer),
        grid=(n_tiles + 1,),
        in_specs=[seg(0), seg(1), seg(2), seg(3),
                  pl.BlockSpec((DEPTH, C_F_DIM), lambda s: (0, 0)),
                  pl.BlockSpec((1, C_VAL_DIM), lambda s: (0, 0)),
                  pl.BlockSpec((None, D_MODEL, D_MODEL), lambda s: (wj, 0, 0), pipeline_mode=pl.Buffered(1)),
                  proj_tile(D_MODEL),
                  pl.BlockSpec((1, D_MODEL), lambda s: (0, 0))],
        out_specs=[proj_tile(D_MODEL), proj_tile(D_MODEL),
                   pl.BlockSpec((1, C_HEADS, C_KEY_DIM, C_VAL_DIM), lambda s: (mix_tile(s) // nt, 0, 0, 0))],
        out_shape=_xn_h_shapes() + [jax.ShapeDtypeStruct((BATCH, C_HEADS, C_KEY_DIM, C_VAL_DIM), _f32)],
        scratch_shapes=[pltpu.VMEM((C_HEADS, C_VAL_DIM, C_KEY_DIM), _f32), g_tile, g_tile,
                        pltpu.VMEM((D_MODEL, D_MODEL), _bf16), pltpu.VMEM((HG_ROWS, D_MODEL), _f32)],
        compiler_params=_cparams(1),
        name="hgrn_prompt",
    )(proj, proj, proj, proj, lb_logits, norm_g, w_out, x, gain)
    return (xn, h), states


def _hgrn_sample_kernel(layer, proj_ref, lg_ref, ng_ref, s_ref, w_ref, x_ref, gain_ref, xn_any, h_any,
                        xn_ref, h_ref, so_ref, g_ref, wb_ref):
    del xn_any, h_any
    step = pl.program_id(0)
    srows = SAMPLE_SEQS * DEC_SEQ
    step_rows = pl.ds(pl.multiple_of(step * srows, srows), srows)
    _round_sample_weight_rows(step, w_ref, wb_ref)
    nt = (((1,), (1,)), ((), ()))
    tn = (((0,), (0,)), ((), ()))
    row = lax.broadcasted_iota(jnp.int32, (DEC_SEQ, DEC_SEQ), 0)
    col = lax.broadcasted_iota(jnp.int32, (DEC_SEQ, DEC_SEQ), 1)
    causal = row >= col
    eye = (lax.broadcasted_iota(jnp.int32, (C_KEY_DIM, C_KEY_DIM), 0)
           == lax.broadcasted_iota(jnp.int32, (C_KEY_DIM, C_KEY_DIM), 1))
    for h in range(C_HEADS):
        hs = slice(h * LANES, (h + 1) * LANES)
        lb = _lower_bound(lg_ref.at[:, hs], layer)
        q, fl, v, z = [proj_ref[:, off + h * LANES:off + (h + 1) * LANES].astype(_f32)
                       for off in (0, C_F_DIM, 2 * C_F_DIM, 2 * C_F_DIM + D_MODEL)]
        qq, kk, lf = _hgrn_gates(q, fl, lb)
        b = _chunk_cumsum(lf, DEC_SEQ)
        q_t = qq * jnp.exp2(b)
        k_intra = kk * jnp.exp2(-b)
        outs = []
        for s in range(SAMPLE_SEQS):
            rows = slice(s * DEC_SEQ, (s + 1) * DEC_SEQ)
            b_last = b[(s + 1) * DEC_SEQ - 1:(s + 1) * DEC_SEQ]
            qb = q_t[rows].astype(_bf16)
            k_state = (kk[rows] * jnp.exp2(b_last - b[rows])).astype(_bf16)
            vb = v[rows].astype(_bf16)
            s0 = s_ref[s, h]
            o_inter = jnp.dot(qb, s0.astype(_bf16), preferred_element_type=_f32)
            att = jnp.where(causal, lax.dot_general(qb, k_intra[rows].astype(_bf16), nt,
                                                    preferred_element_type=_f32), 0.0)
            outs.append(o_inter + jnp.dot(att.astype(_bf16), vb, preferred_element_type=_f32))
            decay = jnp.sum(jnp.where(eye, jnp.broadcast_to(jnp.exp2(b_last), eye.shape), 0.0),
                            axis=1, keepdims=True)
            so_ref[s, h] = decay * s0 + lax.dot_general(k_state, vb, tn, preferred_element_type=_f32)
        g_ref[step_rows, hs] = _hgrn_out(jnp.concatenate(outs, axis=0), z, ng_ref[...]).astype(g_ref.dtype)

    @pl.when(step == pl.num_programs(0) - 1)
    def _():
        _project_sample_rows(False, g_ref, wb_ref, x_ref, gain_ref, (xn_ref, h_ref))


def _hgrn_sample(proj, lb_logits, norm_g, state, layer, w_out, wj, x, gain, into):
    srows = SAMPLE_SEQS * DEC_SEQ
    row0 = P_ROWS // srows
    proj_in_specs, res_specs, res_shapes = _sample_projection_specs(wj, P_ROWS // S_ROWS, into)
    n_in = 7
    state_spec = pl.BlockSpec((SAMPLE_SEQS, C_HEADS, C_KEY_DIM, C_VAL_DIM), lambda b: (b, 0, 0, 0))
    outs = pl.pallas_call(
        functools.partial(_hgrn_sample_kernel, layer),
        grid=(DEC_BATCH // SAMPLE_SEQS,),
        in_specs=[pl.BlockSpec((srows, C_IN_DIM), lambda b: (row0 + b, 0)),
                  pl.BlockSpec((DEPTH, C_F_DIM), lambda b: (0, 0)),
                  pl.BlockSpec((1, C_VAL_DIM), lambda b: (0, 0)),
                  state_spec] + proj_in_specs + [pl.BlockSpec(memory_space=pl.ANY)] * len(into),
        out_specs=res_specs + [state_spec],
        out_shape=res_shapes + [jax.ShapeDtypeStruct((DEC_BATCH, C_HEADS, C_KEY_DIM, C_VAL_DIM), _f32)],
        scratch_shapes=[pltpu.VMEM((S_ROWS, D_MODEL), _bf16), pltpu.VMEM((D_MODEL, D_MODEL), _bf16)],
        input_output_aliases={n_in + k: k for k in range(len(into))},
        compiler_params=_cparams(1),
        name="hgrn_sample",
    )(proj, lb_logits, norm_g, state, w_out, x, gain, *into)
    return outs[:-1], outs[-1]


def kernel(x_prompt, x_sample, cache_k, cache_v, state_conv, state_hgrn, ln_g, final_g, a_w_in, a_w_out, a_sinks,
           b_w_in, b_conv_w, b_w_out, c_w_in, c_norm_g, c_w_out, c_lb_logits):
    xp = x_prompt.reshape(P_ROWS, D_MODEL)
    xs = x_sample.reshape(S_ROWS, D_MODEL)
    tabs = jnp.asarray(_rope_tables())
    norm_gains = [ln_g[li].reshape(1, D_MODEL) for li in range(1, DEPTH)] + [final_g.reshape(1, D_MODEL)]

    h = _first_norm(xp, xs, ln_g[0].reshape(1, D_MODEL))
    x = None
    ks_p, vs_p, conv_p, conv_s, hg_p, hg_s = [], [], [], [], [], []
    kv_sample = None
    caches = [c.reshape(N_A_LAYERS, DEC_BATCH, WINDOW, A_KV_DIM) for c in (cache_k, cache_v)]
    for li in range(DEPTH):
        kind, j = li % 3, li // 3
        if kind == 0:
            qk = _attn_in_proj(h, a_w_in, j, 0, rope_tabs=tabs)
            vz = _attn_in_proj(h, a_w_in, j, A_HALF_DIM)
            final = li == DEPTH - 1
            prompt_res, k_p, v_p = _attn_prompt(qk, vz, a_sinks[j], a_w_out, j, xp if li == 0 else x, norm_gains[li],
                                                final)
            ks_p.append(k_p), vs_p.append(v_p)
            x_s, x_block = (xs, 0) if li == 0 else (x, P_ROWS // S_ROWS)
            res, *kv_sample = _attn_sample(qk, vz, a_sinks[j], caches[0], caches[1], j, a_w_out, x_s, x_block,
                                           norm_gains[li], None if final else prompt_res, kv_sample)
            if final:
                (y_p,), (y_s,) = prompt_res, res
            else:
                x, h = res
            continue
        elif kind == 1:
            g, tail, u_s = _conv_mixer(h, b_w_in, j, b_conv_w[j], *_conv_past_rows(state_conv[j]))
            conv_p.append(tail[:, SUBLANES - (CONV_WIDTH - 1):])
            conv_s.append(u_s.reshape(DEC_BATCH, DEC_SEQ, D_MODEL)[:, DEC_SEQ - (CONV_WIDTH - 1):])
            w_out = b_w_out
        else:
            proj = _in_proj(h, c_w_in, j, C_IN_DIM, MM_COL_TILE)
            ng = c_norm_g[j].reshape(1, C_VAL_DIM)
            res, s_p = _hgrn_prompt(proj, c_lb_logits, ng, li, c_w_out, j, x, norm_gains[li])
            (x, h), s_s = _hgrn_sample(proj, c_lb_logits, ng, state_hgrn[j], li, c_w_out, j, x, norm_gains[li], res)
            hg_p.append(s_p), hg_s.append(s_s)
            continue
        x, h = _out_proj(g, w_out, j, x, norm_gains[li])

    def stacked(parts, *shape):
        arr = parts[0][None] if len(parts) == 1 else jnp.stack(parts)
        return arr.reshape(len(parts), *shape)

    kv_shape = (WINDOW, A_KV_HEADS, A_HEAD_DIM)
    return (y_p.reshape(BATCH, SEQ, D_MODEL), y_s.reshape(DEC_BATCH, DEC_SEQ, D_MODEL),
            stacked(ks_p, BATCH, *kv_shape), stacked(vs_p, BATCH, *kv_shape),
            kv_sample[0].reshape(N_A_LAYERS, DEC_BATCH, *kv_shape),
            kv_sample[1].reshape(N_A_LAYERS, DEC_BATCH, *kv_shape),
            stacked(conv_p, BATCH, CONV_WIDTH - 1, D_MODEL), stacked(conv_s, DEC_BATCH, CONV_WIDTH - 1, D_MODEL),
            stacked(hg_p, BATCH, C_HEADS, C_KEY_DIM, C_VAL_DIM),
            stacked(hg_s, DEC_BATCH, C_HEADS, C_KEY_DIM, C_VAL_DIM))
```
